```python
import jax, jax.numpy as jnp
from jax import lax
import numpy as np

D_MODEL = 1024
BATCH = 2
SEQ = 16384
DEPTH = 1
DEC_BATCH = 32
DEC_SEQ = 32
PAST_LEN = 4096

CHUNK = 64
Q_BLOCK = 128
HEAD_DIM = 64
N_HEADS_A = (D_MODEL // 2) // HEAD_DIM
N_KV_A = 2
N_HEADS_B = (D_MODEL // 2) // HEAD_DIM
N_IDX_HEADS = 8
IDX_DIM = 32
TOPK_MAX = 256
D_FF = 4 * D_MODEL
PLE_DIM = 256
ROPE_THETA = 500000.0
EPS = 1e-6
WIDTH_A = N_HEADS_A * HEAD_DIM
WIDTH_B = N_HEADS_B * HEAD_DIM
IN_SIZES = (WIDTH_A, N_KV_A * HEAD_DIM, N_KV_A * HEAD_DIM, N_IDX_HEADS * IDX_DIM, IDX_DIM, N_IDX_HEADS,
            WIDTH_B, WIDTH_B, WIDTH_B, D_MODEL, D_MODEL)
IN_WIDTH = sum(IN_SIZES)

kernel_name = 'dsa_stickbreaking_gated_hybrid_step'


def _split_offsets():
    offs = []
    acc = 0
    for s in IN_SIZES[:-1]:
        acc += s
        offs.append(acc)
    return offs


def rmsnorm(x, g):
    xf = x.astype(jnp.float32)
    y = xf * lax.rsqrt(jnp.mean(xf * xf, axis=-1, keepdims=True) + EPS)
    return y.astype(x.dtype) * g


def rope_partial(x, pos):
    d = x.shape[-1]
    r = d // 4
    hr = r // 2
    inv = jnp.power(jnp.float32(ROPE_THETA), -jnp.arange(hr, dtype=jnp.float32) * (2.0 / r))
    ang = pos.astype(jnp.float32)[:, None] * inv[None, :]
    ang = ang.reshape((1, pos.shape[0]) + (1,) * (x.ndim - 3) + (hr,))
    cos, sin = jnp.cos(ang), jnp.sin(ang)
    xr = x[..., :r].astype(jnp.float32)
    x1, x2 = xr[..., :hr], xr[..., hr:]
    rot = jnp.concatenate([x1 * cos - x2 * sin, x2 * cos + x1 * sin], axis=-1)
    return jnp.concatenate([rot.astype(x.dtype), x[..., r:]], axis=-1)


def mixer_inputs(h, pos, w_in, g_qa, g_ka):
    B, T, _ = h.shape
    u = h @ w_in
    qa, ka, va, qi, ki, wi, qb, kb, vb, ga, gb = jnp.split(u, _split_offsets(), axis=-1)
    qa = rope_partial(rmsnorm(qa.reshape(B, T, N_HEADS_A, HEAD_DIM), g_qa), pos)
    ka = rope_partial(rmsnorm(ka.reshape(B, T, N_KV_A, HEAD_DIM), g_ka), pos)
    va = va.reshape(B, T, N_KV_A, HEAD_DIM)
    qi = rope_partial(qi.reshape(B, T, N_IDX_HEADS, IDX_DIM), pos)
    ki = rope_partial(ki, pos)
    wi = wi * (N_IDX_HEADS ** -0.5)
    qb = qb.reshape(B, T, N_HEADS_B, HEAD_DIM)
    kb = kb.reshape(B, T, N_HEADS_B, HEAD_DIM)
    vb = vb.reshape(B, T, N_HEADS_B, HEAD_DIM)
    return qa, ka, va, qi, ki, wi, qb, kb, vb, ga, gb


def dsa_block(q, qi, wi, q_pos, k, v, ki, n_sel):
    B, T, H, D = q.shape
    L = k.shape[1]
    logits = jnp.einsum('bthi,bsi->bths', qi, ki).astype(jnp.float32) * (IDX_DIM ** -0.5)
    score = jnp.einsum('bths,bth->bts', jax.nn.relu(logits), wi.astype(jnp.float32))
    limit = (q_pos // CHUNK + 1) * CHUNK
    k_pos = jnp.arange(L, dtype=jnp.int32)
    score = jnp.where((k_pos[None, :] < limit[:, None])[None], score, -jnp.inf)
    _, idx = lax.top_k(score, n_sel)
    valid = idx < limit[None, :, None]
    gather = jax.vmap(lambda rows, ids: rows[ids])
    k_sel = gather(k, idx)
    v_sel = gather(v, idx)
    qg = q.reshape(B, T, N_KV_A, H // N_KV_A, D)
    s = jnp.einsum('btkgd,btnkd->btkgn', qg, k_sel).astype(jnp.float32) * (D ** -0.5)
    s = jnp.where(valid[:, :, None, None, :], s, -jnp.inf)
    pr = jax.nn.softmax(s, axis=-1).astype(v.dtype)
    o = jnp.einsum('btkgn,btnkd->btkgd', pr, v_sel)
    return o.reshape(B, T, H * D)


def stick_breaking_block(q, q_pos, k, v):
    B, T, H, D = q.shape
    L = k.shape[1]
    z = jnp.einsum('bthd,bshd->bhts', q, k).astype(jnp.float32) * (D ** -0.5)
    mask = jnp.arange(L, dtype=jnp.int32)[None, :] < q_pos[:, None]
    log_beta = jax.nn.log_sigmoid(z)
    log_keep = jnp.where(mask, jax.nn.log_sigmoid(-z), 0.0)
    incl = lax.cumsum(log_keep, axis=3, reverse=True)
    excl = jnp.concatenate([incl[..., 1:], jnp.zeros_like(incl[..., :1])], axis=-1)
    a = jnp.where(mask, jnp.exp(log_beta + excl), 0.0).astype(v.dtype)
    o = jnp.einsum('bhts,bshd->bthd', a, v)
    return o.reshape(B, T, H * D)


def to_blocks(a):
    B, T = a.shape[:2]
    return jnp.moveaxis(a.reshape((B, T // Q_BLOCK, Q_BLOCK) + a.shape[2:]), 1, 0)


def from_blocks(a):
    NB, B, QB = a.shape[:3]
    return jnp.moveaxis(a, 0, 1).reshape((B, NB * QB) + a.shape[3:])


def merge(o_a, o_b, ga, gb, proj_a, proj_b, w_out):
    m = jax.nn.sigmoid(ga) * (o_a @ proj_a) + jax.nn.sigmoid(gb) * (o_b @ proj_b)
    return m @ w_out


def channel_mlp(x, g, w_up, w_down):
    return jnp.square(jax.nn.relu(rmsnorm(x, g) @ w_up)) @ w_down


def per_layer_embed(x, p, g, w_ple, w_gate):
    return (p @ w_ple) * jax.nn.sigmoid(rmsnorm(x, g) @ w_gate)


def setup_inputs(seed: int = 0) -> dict:
    key = jax.random.key(seed)
    ks = jax.random.split(key, 24)
    f32 = jnp.float32

    def nrm(k, shape, scale=1.0):
        return jax.random.normal(k, shape, f32) * scale

    def gain(k, shape):
        return 1.0 + 0.01 * jax.random.normal(k, shape, f32)

    return {
        'x_prompt': nrm(ks[0], (BATCH, SEQ, D_MODEL)),
        'x_sample': nrm(ks[1], (DEC_BATCH, DEC_SEQ, D_MODEL)),
        'cache_a_k': nrm(ks[2], (DEPTH, DEC_BATCH, PAST_LEN, N_KV_A, HEAD_DIM)),
        'cache_a_v': nrm(ks[3], (DEPTH, DEC_BATCH, PAST_LEN, N_KV_A, HEAD_DIM)),
        'cache_a_kidx': nrm(ks[4], (DEPTH, DEC_BATCH, PAST_LEN, IDX_DIM)),
        'cache_b_k': nrm(ks[5], (DEPTH, DEC_BATCH, PAST_LEN, N_HEADS_B, HEAD_DIM)),
        'cache_b_v': nrm(ks[6], (DEPTH, DEC_BATCH, PAST_LEN, N_HEADS_B, HEAD_DIM)),
        'p_prompt': nrm(ks[7], (DEPTH, BATCH, SEQ, PLE_DIM)),
        'p_sample': nrm(ks[8], (DEPTH, DEC_BATCH, DEC_SEQ, PLE_DIM)),
        'norm_mix': gain(ks[9], (DEPTH, D_MODEL)),
        'w_in': nrm(ks[10], (DEPTH, D_MODEL, IN_WIDTH), D_MODEL ** -0.5),
        'g_qa': gain(ks[11], (DEPTH, HEAD_DIM)),
        'g_ka': gain(ks[12], (DEPTH, HEAD_DIM)),
        'proj_a': nrm(ks[13], (DEPTH, WIDTH_A, D_MODEL), WIDTH_A ** -0.5),
        'proj_b': nrm(ks[14], (DEPTH, WIDTH_B, D_MODEL), WIDTH_B ** -0.5),
        'w_out': nrm(ks[15], (DEPTH, D_MODEL, D_MODEL), D_MODEL ** -0.5),
        'norm_ffn': gain(ks[16], (DEPTH, D_MODEL)),
        'w_up': nrm(ks[17], (DEPTH, D_MODEL, D_FF), D_MODEL ** -0.5),
        'w_down': nrm(ks[18], (DEPTH, D_FF, D_MODEL), D_FF ** -0.5),
        'norm_ple': gain(ks[19], (DEPTH, D_MODEL)),
        'w_ple': nrm(ks[20], (DEPTH, PLE_DIM, D_MODEL), PLE_DIM ** -0.5),
        'w_ple_gate': nrm(ks[21], (DEPTH, D_MODEL, D_MODEL), D_MODEL ** -0.5),
    }


def reference(x_prompt, x_sample, cache_a_k, cache_a_v, cache_a_kidx, cache_b_k, cache_b_v,
              p_prompt, p_sample, norm_mix, w_in, g_qa, g_ka, proj_a, proj_b, w_out,
              norm_ffn, w_up, w_down, norm_ple, w_ple, w_ple_gate):
    seq = x_prompt.shape[1]
    dec_seq = x_sample.shape[1]
    past = cache_a_k.shape[2]
    pos_p = jnp.arange(seq, dtype=jnp.int32)
    pos_p_blocks = pos_p.reshape(seq // Q_BLOCK, Q_BLOCK)
    pos_s = past + jnp.arange(dec_seq, dtype=jnp.int32)
    n_sel_p = min(TOPK_MAX, seq // 4)
    n_sel_s = min(TOPK_MAX, (past + dec_seq) // 4)

    xp, xs = x_prompt, x_sample
    pak, pav, paki, pbk, pbv = [], [], [], [], []
    sak, sav, saki, sbk, sbv = [], [], [], [], []
    for i in range(DEPTH):
        h = rmsnorm(xp, norm_mix[i])
        qa, ka, va, qi, ki, wi, qb, kb, vb, ga, gb = mixer_inputs(h, pos_p, w_in[i], g_qa[i], g_ka[i])
        o_a = from_blocks(lax.map(
            lambda t: dsa_block(t[0], t[1], t[2], t[3], ka, va, ki, n_sel_p),
            (to_blocks(qa), to_blocks(qi), to_blocks(wi), pos_p_blocks)))
        o_b = from_blocks(lax.map(
            lambda t: stick_breaking_block(t[0], t[1], kb, vb),
            (to_blocks(qb), pos_p_blocks)))
        xp = xp + merge(o_a, o_b, ga, gb, proj_a[i], proj_b[i], w_out[i])
        xp = xp + channel_mlp(xp, norm_ffn[i], w_up[i], w_down[i])
        xp = xp + per_layer_embed(xp, p_prompt[i], norm_ple[i], w_ple[i], w_ple_gate[i])
        pak.append(ka); pav.append(va); paki.append(ki); pbk.append(kb); pbv.append(vb)

        h = rmsnorm(xs, norm_mix[i])
        qa, ka, va, qi, ki, wi, qb, kb, vb, ga, gb = mixer_inputs(h, pos_s, w_in[i], g_qa[i], g_ka[i])
        ka_all = jnp.concatenate([cache_a_k[i], ka], axis=1)
        va_all = jnp.concatenate([cache_a_v[i], va], axis=1)
        ki_all = jnp.concatenate([cache_a_kidx[i], ki], axis=1)
        kb_all = jnp.concatenate([cache_b_k[i], kb], axis=1)
        vb_all = jnp.concatenate([cache_b_v[i], vb], axis=1)
        o_a = dsa_block(qa, qi, wi, pos_s, ka_all, va_all, ki_all, n_sel_s)
        o_b = stick_breaking_block(qb, pos_s, kb_all, vb_all)
        xs = xs + merge(o_a, o_b, ga, gb, proj_a[i], proj_b[i], w_out[i])
        xs = xs + channel_mlp(xs, norm_ffn[i], w_up[i], w_down[i])
        xs = xs + per_layer_embed(xs, p_sample[i], norm_ple[i], w_ple[i], w_ple_gate[i])
        sak.append(ka); sav.append(va); saki.append(ki); sbk.append(kb); sbv.append(vb)

    new_a_k_p = jnp.stack(pak, 0)
    new_a_v_p = jnp.stack(pav, 0)
    new_a_kidx_p = jnp.stack(paki, 0)
    new_b_k_p = jnp.stack(pbk, 0)
    new_b_v_p = jnp.stack(pbv, 0)
    new_a_k_s = jnp.stack(sak, 0)
    new_a_v_s = jnp.stack(sav, 0)
    new_a_kidx_s = jnp.stack(saki, 0)
    new_b_k_s = jnp.stack(sbk, 0)
    new_b_v_s = jnp.stack(sbv, 0)
    return (xp, xs, new_a_k_p, new_a_v_p, new_a_kidx_p, new_b_k_p, new_b_v_p,
            new_a_k_s, new_a_v_s, new_a_kidx_s, new_b_k_s, new_b_v_s)
```

```python
import functools

import numpy as np
import jax
import jax.numpy as jnp
from jax import lax
from jax.experimental import pallas as pl
from jax.experimental.pallas import tpu as pltpu

CHUNK = 64
HEAD_DIM = 64
N_HEADS_A = 8
N_KV_A = 2
GROUP_A = N_HEADS_A // N_KV_A
N_HEADS_B = 8
N_IDX_HEADS = 8
IDX_DIM = 32
TOPK_MAX = 256
ROPE_THETA = 500000.0
EPS = 1e-6
WIDTH_A = N_HEADS_A * HEAD_DIM
WIDTH_B = N_HEADS_B * HEAD_DIM
KV_A = N_KV_A * HEAD_DIM

LANES = 128
SUBLANES = 8
VMEM_LIMIT_BYTES = 56 * 1024 * 1024

TOKEN_TILE = 256
KEY_TILE = 256
Q_TILE_A = 128
Q_TILE_B = 256

INT_MIN = -2147483648
MASKED = -1e30
STICK_CUTOFF = -120.0
STICK_BOUND_SLACK = 1.01

F32 = jnp.float32
BF16 = jnp.bfloat16
I32 = jnp.int32


def _const_spec(shape):
    nd = len(shape)
    return pl.BlockSpec(shape, lambda *_: (0,) * nd, pipeline_mode=pl.Buffered(1))


def _dot(a, b):
    return jnp.dot(a, b, preferred_element_type=F32)


def _dot_hilo(x, m):
    hi = x.astype(BF16)
    lo = (x - hi.astype(F32)).astype(BF16)
    return _dot(hi, m) + _dot(lo, m)


_SEG = {}
_off = 0
for _name, _w in (("qa", WIDTH_A), ("ka", KV_A), ("va", KV_A), ("qi", N_IDX_HEADS * IDX_DIM),
                  ("kiwi", LANES), ("qb", WIDTH_B), ("kb", WIDTH_B), ("vb", WIDTH_B),
                  ("ga", None), ("gb", None)):
    _SEG[_name] = (_off, _w)
    _off += _w if _w is not None else 0
del _off, _name, _w


def _rope(y, c, s1, s2, half):
    cols = []
    for k in range(y.shape[1] // LANES):
        yk = y[:, k * LANES:(k + 1) * LANES]
        cols.append(yk * c + pltpu.roll(yk, LANES - half, 1) * s1 + pltpu.roll(yk, half, 1) * s2)
    return cols[0] if len(cols) == 1 else jnp.concatenate(cols, axis=1)


def _proj_kernel(x_ref, g_ref, w_ref, gq_ref, gk_ref, gm_ref, tab_ref,
                 qa_ref, ka_ref, va_ref, qi_ref, kiwi_ref, qb_ref, kb_ref, vb_ref, ga_ref, gb_ref,
                 *, d_model):
    x = x_ref[...]
    ms = jnp.mean(x * x, axis=-1, keepdims=True)
    h = ((x * lax.rsqrt(ms + EPS)) * g_ref[...]).astype(BF16)

    def seg(name, width=None):
        off, w = _SEG[name]
        w = width if w is None else w
        return _dot(h, w_ref[:, off:off + w])

    def tab(k):
        return tab_ref[:, k * LANES:(k + 1) * LANES]

    def head_norm(u, gain):
        w = u.shape[1]
        msq = _dot_hilo(u * u, gm_ref[:w, :w])
        return (u * lax.rsqrt(msq + EPS)) * gain

    qa = head_norm(seg("qa"), gq_ref[...])
    qa = _rope(qa, tab(0), tab(1), tab(2), HEAD_DIM // 8)
    qa_ref[...] = (qa * (HEAD_DIM ** -0.5)).astype(BF16)
    ka = head_norm(seg("ka"), gk_ref[...])
    ka_ref[...] = _rope(ka, tab(0), tab(1), tab(2), HEAD_DIM // 8)
    va_ref[...] = seg("va")
    qi = _rope(seg("qi"), tab(3), tab(4), tab(5), IDX_DIM // 8)
    qi_ref[...] = qi.astype(BF16)
    kiwi_ref[...] = _rope(seg("kiwi"), tab(6), tab(7), tab(8), IDX_DIM // 8)
    qb_ref[...] = (seg("qb") * (HEAD_DIM ** -0.5)).astype(BF16)
    kb_ref[...] = seg("kb")
    vb_ref[...] = seg("vb")
    ga_off = _SEG["ga"][0]
    ga_ref[...] = _dot(h, w_ref[:, ga_off:ga_off + d_model])
    gb_ref[...] = _dot(h, w_ref[:, ga_off + d_model:ga_off + 2 * d_model])


def _rope_tables(pos):
    def one(d):
        r = d // 4
        hr = r // 2
        inv = jnp.power(jnp.float32(ROPE_THETA), -jnp.arange(hr, dtype=F32) * (2.0 / r))
        ang = pos.astype(F32)[:, None] * inv[None, :]
        cos, sin = jnp.cos(ang), jnp.sin(ang)
        t = pos.shape[0]
        one_ = jnp.ones((t, d - r), F32)
        zero = jnp.zeros((t, d - r), F32)
        zh = jnp.zeros((t, hr), F32)
        c = jnp.concatenate([cos, cos, one_], axis=1)
        s1 = jnp.concatenate([-sin, zh, zero], axis=1)
        s2 = jnp.concatenate([zh, sin, zero], axis=1)
        return c, s1, s2

    c64, s164, s264 = one(HEAD_DIM)
    c32, s132, s232 = one(IDX_DIM)
    t = pos.shape[0]
    rep = lambda a, d: jnp.tile(a, (1, LANES // d))
    padc = jnp.ones((t, LANES - IDX_DIM), F32)
    padz = jnp.zeros((t, LANES - IDX_DIM), F32)
    return jnp.concatenate([
        rep(c64, HEAD_DIM), rep(s164, HEAD_DIM), rep(s264, HEAD_DIM),
        rep(c32, IDX_DIM), rep(s132, IDX_DIM), rep(s232, IDX_DIM),
        jnp.concatenate([c32, padc], axis=1), jnp.concatenate([s132, padz], axis=1),
        jnp.concatenate([s232, padz], axis=1)], axis=1)


def _project(x2d, seq_len, pos, norm_mix, w_in_p, g_qa, g_ka):
    n, d_model = x2d.shape
    tm = min(TOKEN_TILE, n)
    assert n % tm == 0 and (seq_len % tm == 0 or tm % seq_len == 0)
    tables = _rope_tables(pos)
    if seq_len < tm:
        tables = jnp.tile(tables, (tm // seq_len, 1))
    tiles_per_seq = max(seq_len // tm, 1)
    gm = jnp.asarray(np.kron(np.eye(N_HEADS_A, dtype=np.float32),
                             np.full((HEAD_DIM, HEAD_DIM), 1.0 / HEAD_DIM, np.float32)), BF16)
    gq = jnp.tile(g_qa.reshape(1, HEAD_DIM), (1, N_HEADS_A))
    gk = jnp.tile(g_ka.reshape(1, HEAD_DIM), (1, N_KV_A))
    wcols = w_in_p.shape[1]

    row = lambda w: pl.BlockSpec((tm, w), lambda i: (i, 0))
    out_shapes = [
        jax.ShapeDtypeStruct((n, WIDTH_A), BF16), jax.ShapeDtypeStruct((n, KV_A), F32),
        jax.ShapeDtypeStruct((n, KV_A), F32), jax.ShapeDtypeStruct((n, N_IDX_HEADS * IDX_DIM), BF16),
        jax.ShapeDtypeStruct((n, LANES), F32), jax.ShapeDtypeStruct((n, WIDTH_B), BF16),
        jax.ShapeDtypeStruct((n, WIDTH_B), F32), jax.ShapeDtypeStruct((n, WIDTH_B), F32),
        jax.ShapeDtypeStruct((n, d_model), F32), jax.ShapeDtypeStruct((n, d_model), F32)]
    return pl.pallas_call(
        functools.partial(_proj_kernel, d_model=d_model),
        grid=(n // tm,),
        in_specs=[row(d_model), _const_spec((1, d_model)), _const_spec((d_model, wcols)),
                  _const_spec((1, WIDTH_A)), _const_spec((1, KV_A)), _const_spec((WIDTH_A, WIDTH_A)),
                  pl.BlockSpec((tm, 9 * LANES), lambda i: (i % tiles_per_seq, 0))],
        out_specs=[row(s.shape[1]) for s in out_shapes],
        out_shape=out_shapes,
        compiler_params=pltpu.CompilerParams(dimension_semantics=("arbitrary",),
                                             vmem_limit_bytes=VMEM_LIMIT_BYTES),
        name="proj",
    )(x2d, norm_mix.reshape(1, d_model), w_in_p, gq, gk, gm, tables)


def _dsa_kernel(qi_ref, w_ref, qa_ref, ki_ref, ka_ref, va_ref, o_ref,
                keys_ref, m_ref, l_ref, acc_ref, *, pos0, n_keys, n_sel, tq, tk, idx_bits):
    i = pl.program_id(1)
    pos_first = pos0 + i * tq
    lim_max = jnp.minimum(((pos_first + tq - 1) // CHUNK + 1) * CHUNK, n_keys)
    n_tiles = (lim_max + tk - 1) // tk
    lane_pos = pos_first + lax.broadcasted_iota(I32, (1, tq), 1)
    lim_row = jnp.minimum((lax.shift_right_arithmetic(lane_pos, int(np.log2(CHUNK))) + 1) * CHUNK, n_keys)
    row_iota = lax.broadcasted_iota(I32, (tk, tq), 0)

    qi = qi_ref[...]
    w = w_ref[...]

    def score_tile(j, _):
        r0 = pl.multiple_of(j * tk, tk)
        lg = _dot(ki_ref[pl.ds(r0, tk), :], qi)
        sc = jnp.zeros((tk, tq), F32)
        for h in range(N_IDX_HEADS):
            sc = sc + jnp.maximum(lg[:, h * tq:(h + 1) * tq], 0.0) * w[h:h + 1, :]
        bits = pltpu.bitcast(sc, I32)
        key = bits ^ (lax.shift_right_arithmetic(bits, 31) & 0x7FFFFFFF)
        key = jnp.where(row_iota + r0 < lim_row, key, INT_MIN)
        keys_ref[pl.ds(r0, tk), :] = key
        return 0

    lax.fori_loop(0, n_tiles, score_tile, 0)

    def count(pred):
        def body(j, acc):
            r0 = pl.multiple_of(j * tk, tk)
            hit = jnp.where(pred(keys_ref[pl.ds(r0, tk), :], r0), 1, 0).astype(I32)
            part = hit[0:SUBLANES]
            for r in range(1, tk // SUBLANES):
                part = part + hit[r * SUBLANES:(r + 1) * SUBLANES]
            return acc + part
        acc = lax.fori_loop(0, n_tiles, body, jnp.zeros((SUBLANES, tq), I32))
        return jnp.sum(acc, axis=0, keepdims=True)

    def bisect(t, thr):
        cand = thr ^ lax.shift_left(jnp.int32(1), 31 - t)
        cnt = count(lambda kk, r0: kk >= cand)
        return jnp.where(cnt >= n_sel, cand, thr)

    thr = lax.fori_loop(0, 32, bisect, jnp.full((1, tq), INT_MIN, I32))
    c_ge = count(lambda kk, r0: kk >= thr)
    c_gt = count(lambda kk, r0: kk > thr)
    need = n_sel - c_gt
    tie = jnp.where(c_ge - c_gt > need, jnp.where(thr > INT_MIN, 1, 0), 0).astype(I32)

    def tie_cut():
        def step(t, c0):
            cand = c0 | lax.shift_left(jnp.int32(1), idx_bits - 1 - t)
            cnt = count(lambda kk, r0: jnp.where(kk == thr, row_iota + r0, cand) < cand)
            return jnp.where(cnt < need, cand, c0)
        return lax.fori_loop(0, idx_bits, step, jnp.zeros((1, tq), I32))

    any_tie = jnp.max(tie) > 0
    cut = lax.cond(any_tie, tie_cut, lambda: jnp.zeros((1, tq), I32))
    cut = jnp.where(thr > INT_MIN, jnp.where(tie > 0, cut, jnp.int32(2 ** 30)), -1)

    m_ref[...] = jnp.full(m_ref.shape, MASKED, F32)
    l_ref[...] = jnp.zeros(l_ref.shape, F32)
    acc_ref[...] = jnp.zeros(acc_ref.shape, F32)

    def attend_tile(j, _):
        r0 = pl.multiple_of(j * tk, tk)
        kk = keys_ref[pl.ds(r0, tk), :]
        bias = jnp.where(kk > thr, 0.0,
                         jnp.where(kk == thr, jnp.where(row_iota + r0 <= cut, 0.0, MASKED), MASKED))
        bias = jnp.concatenate([bias] * GROUP_A, axis=1)
        for g in range(N_KV_A):
            s = _dot(ka_ref[g, pl.ds(r0, tk), :], qa_ref[g]) + bias
            m_old = m_ref[g]
            m_new = jnp.maximum(m_old, jnp.max(s, axis=0, keepdims=True))
            alpha = jnp.exp(m_old - m_new)
            p = jnp.exp(s - m_new)
            l_ref[g] = alpha * l_ref[g] + jnp.sum(p, axis=0, keepdims=True)
            acc_ref[g] = acc_ref[g] * alpha + _dot(va_ref[g, j], p.astype(BF16))
            m_ref[g] = m_new
        return 0

    lax.fori_loop(0, n_tiles, attend_tile, 0)
    for g in range(N_KV_A):
        o_ref[g] = acc_ref[g] / l_ref[g]


def _dsa(qi_t, w_t, qa_t, ki, ka, va_t, *, pos0, n_keys, n_sel, tq, tk):
    b, nq = qi_t.shape[:2]
    lp = ki.shape[1]
    nt = lp // tk
    sq = None
    kern = functools.partial(_dsa_kernel, pos0=pos0, n_keys=n_keys, n_sel=n_sel, tq=tq, tk=tk,
                             idx_bits=int(np.ceil(np.log2(lp))))
    per_batch = lambda shape: pl.BlockSpec((sq,) + shape, lambda bi, i: (bi,) + (0,) * len(shape),
                                           pipeline_mode=pl.Buffered(1))
    per_q = lambda shape: pl.BlockSpec((sq, sq) + shape, lambda bi, i: (bi, i) + (0,) * len(shape))
    return pl.pallas_call(
        kern,
        grid=(b, nq),
        in_specs=[per_q((IDX_DIM, N_IDX_HEADS * tq)), per_q((N_IDX_HEADS, tq)),
                  per_q((N_KV_A, HEAD_DIM, GROUP_A * tq)),
                  per_batch((lp, IDX_DIM)), per_batch((N_KV_A, lp, HEAD_DIM)),
                  per_batch((N_KV_A, nt, HEAD_DIM, tk))],
        out_specs=per_q((N_KV_A, HEAD_DIM, GROUP_A * tq)),
        out_shape=jax.ShapeDtypeStruct((b, nq, N_KV_A, HEAD_DIM, GROUP_A * tq), F32),
        scratch_shapes=[pltpu.VMEM((lp, tq), I32),
                        pltpu.VMEM((N_KV_A, 1, GROUP_A * tq), F32),
                        pltpu.VMEM((N_KV_A, 1, GROUP_A * tq), F32),
                        pltpu.VMEM((N_KV_A, HEAD_DIM, GROUP_A * tq), F32)],
        compiler_params=pltpu.CompilerParams(dimension_semantics=("arbitrary", "arbitrary"),
                                             vmem_limit_bytes=VMEM_LIMIT_BYTES),
        name="dsa",
    )(qi_t, w_t, qa_t, ki, ka, va_t)


def _stick_kernel(q_ref, k_ref, v_ref, tri_ref, o_ref, kmax_ref, *, pos0, tq, tk, n_key_rows):
    i = pl.program_id(2)

    @pl.when(i == 0)
    def _():
        def body(j, mx):
            r0 = pl.multiple_of(j * tk, tk)
            kf = k_ref[pl.ds(r0, tk), :].astype(F32)
            sq = jnp.sum(kf * kf, axis=1, keepdims=True)
            return jnp.maximum(mx, jnp.max(sq, axis=0, keepdims=True))
        mx = lax.fori_loop(0, n_key_rows // tk, body, jnp.zeros((1, 1), F32))
        kmax_ref[...] = jnp.sqrt(jnp.broadcast_to(mx, kmax_ref.shape))

    q = q_ref[...]
    qf = q.astype(F32)
    qn = jnp.sqrt(jnp.sum(qf * qf, axis=0, keepdims=True))
    bound = qn * kmax_ref[0:1, 0:1] * STICK_BOUND_SLACK
    pos_first = pos0 + i * tq
    pos_row = pos_first + lax.broadcasted_iota(I32, (1, tq), 1)
    row_iota = lax.broadcasted_iota(I32, (tk, tq), 0)
    j_first = jnp.maximum(pos_first + tq - 2, 0) // tk
    tri = tri_ref[...]

    def cond(state):
        j, live, _, _ = state
        return jnp.logical_and(j >= 0, live > 0)

    def body(state):
        j, _, carry, acc = state
        r0 = pl.multiple_of(j * tk, tk)
        z = _dot(k_ref[pl.ds(r0, tk), :], q)
        vis = row_iota + r0 < pos_row
        softplus = jnp.maximum(z, 0.0) + jnp.log(1.0 + jnp.exp(-jnp.abs(z)))
        log_keep = jnp.where(vis, -softplus, 0.0)
        incl = _dot_hilo_left(tri, log_keep)
        a = jnp.where(vis, jnp.exp(z + incl + carry), 0.0)
        acc = acc + _dot(v_ref[j], a.astype(BF16))
        carry = carry + incl[0:1, :]
        live = (jnp.max(bound + carry) >= STICK_CUTOFF).astype(I32)
        return j - 1, live, carry, acc

    init = (j_first, jnp.int32(1), jnp.zeros((1, tq), F32), jnp.zeros((HEAD_DIM, tq), F32))
    o_ref[...] = lax.while_loop(cond, body, init)[3]


def _dot_hilo_left(m, x):
    hi = x.astype(BF16)
    lo = (x - hi.astype(F32)).astype(BF16)
    return _dot(m, hi) + _dot(m, lo)


def _stick(q_t, k, v_t, *, pos0, tq, tk):
    b, nh, _, t = q_t.shape
    lp = k.shape[2]
    nt = lp // tk
    sq = None
    tri = jnp.asarray(np.triu(np.ones((tk, tk), np.float32)), BF16)
    kern = functools.partial(_stick_kernel, pos0=pos0, tq=tq, tk=tk, n_key_rows=lp)
    return pl.pallas_call(
        kern,
        grid=(b, nh, t // tq),
        in_specs=[pl.BlockSpec((sq, sq, HEAD_DIM, tq), lambda bi, h, i: (bi, h, 0, i)),
                  pl.BlockSpec((sq, sq, lp, HEAD_DIM), lambda bi, h, i: (bi, h, 0, 0)),
                  pl.BlockSpec((sq, sq, nt, HEAD_DIM, tk), lambda bi, h, i: (bi, h, 0, 0, 0)),
                  _const_spec((tk, tk))],
        out_specs=pl.BlockSpec((sq, sq, HEAD_DIM, tq), lambda bi, h, i: (bi, h, 0, i)),
        out_shape=jax.ShapeDtypeStruct((b, nh, HEAD_DIM, t), F32),
        scratch_shapes=[pltpu.VMEM((SUBLANES, LANES), F32)],
        compiler_params=pltpu.CompilerParams(dimension_semantics=("arbitrary",) * 3,
                                             vmem_limit_bytes=VMEM_LIMIT_BYTES),
        name="stick",
    )(q_t, k, v_t, tri)


def _rms(x, g):
    ms = jnp.mean(x * x, axis=-1, keepdims=True)
    return (x * lax.rsqrt(ms + EPS)) * g


def _tail_kernel(x_ref, oa_ref, ob_ref, ga_ref, gb_ref, p_ref, pa_ref, pb_ref, wo_ref,
                 nf_ref, wu_ref, wd_ref, np_ref, wp_ref, wg_ref, y_ref, *, ff_chunk):
    mix = (jax.nn.sigmoid(ga_ref[...]) * _dot(oa_ref[...], pa_ref[...])
           + jax.nn.sigmoid(gb_ref[...]) * _dot(ob_ref[...], pb_ref[...]))
    x = x_ref[...] + _dot(mix.astype(BF16), wo_ref[...])
    h = _rms(x, nf_ref[...]).astype(BF16)
    d_ff = wu_ref.shape[1]
    ffn = jnp.zeros_like(x)
    for c in range(d_ff // ff_chunk):
        u = jnp.maximum(_dot(h, wu_ref[:, c * ff_chunk:(c + 1) * ff_chunk]), 0.0)
        ffn = ffn + _dot((u * u).astype(BF16), wd_ref[c * ff_chunk:(c + 1) * ff_chunk, :])
    x = x + ffn
    h = _rms(x, np_ref[...]).astype(BF16)
    y_ref[...] = x + _dot(p_ref[...].astype(BF16), wp_ref[...]) * jax.nn.sigmoid(_dot(h, wg_ref[...]))


def _tail(x2d, o_a, o_b, ga, gb, p2d, proj_a, proj_b, w_out, norm_ffn, w_up, w_down,
          norm_ple, w_ple, w_gate):
    n, d_model = x2d.shape
    tm = min(TOKEN_TILE, n)
    assert n % tm == 0
    d_ff = w_up.shape[1]
    row = lambda w: pl.BlockSpec((tm, w), lambda i: (i, 0))
    weights = (proj_a, proj_b, w_out, norm_ffn.reshape(1, d_model), w_up, w_down,
               norm_ple.reshape(1, d_model), w_ple, w_gate)
    return pl.pallas_call(
        functools.partial(_tail_kernel, ff_chunk=min(1024, d_ff)),
        grid=(n // tm,),
        in_specs=[row(d_model), row(WIDTH_A), row(WIDTH_B), row(d_model), row(d_model),
                  row(p2d.shape[1])] + [_const_spec(w.shape) for w in weights],
        out_specs=row(d_model),
        out_shape=jax.ShapeDtypeStruct((n, d_model), F32),
        compiler_params=pltpu.CompilerParams(dimension_semantics=("arbitrary",),
                                             vmem_limit_bytes=VMEM_LIMIT_BYTES),
        name="tail",
    )(x2d, o_a, o_b, ga, gb, p2d, *weights)


def _pad_keys(a, lp):
    return jnp.pad(a, ((0, 0), (0, lp - a.shape[1])) + ((0, 0),) * (a.ndim - 2))


def _layer(x, p, pos0, caches, wts):
    (norm_mix, w_in_p, g_qa, g_ka, proj_a, proj_b, w_out, norm_ffn, w_up, w_down,
     norm_ple, w_ple, w_gate) = wts
    b, t, d_model = x.shape
    n = b * t
    pos = pos0 + jnp.arange(t, dtype=I32)
    x2d = x.reshape(n, d_model)
    qa, ka, va, qi, kiwi, qb, kb, vb, ga, gb = _project(x2d, t, pos, norm_mix, w_in_p, g_qa, g_ka)

    ka = ka.reshape(b, t, N_KV_A, HEAD_DIM)
    va = va.reshape(b, t, N_KV_A, HEAD_DIM)
    ki = kiwi[:, :IDX_DIM].reshape(b, t, IDX_DIM)
    kb = kb.reshape(b, t, N_HEADS_B, HEAD_DIM)
    vb = vb.reshape(b, t, N_HEADS_B, HEAD_DIM)
    new_state = (ka, va, ki, kb, vb)
    wi = kiwi[:, IDX_DIM:IDX_DIM + N_IDX_HEADS] * ((N_IDX_HEADS ** -0.5) * (IDX_DIM ** -0.5))

    if caches is not None:
        ka_all, va_all, ki_all, kb_all, vb_all = (
            jnp.concatenate([c.astype(BF16), s.astype(BF16)], axis=1) for c, s in zip(caches, new_state))
    else:
        ka_all, va_all, ki_all, kb_all, vb_all = (s.astype(BF16) for s in new_state)
    n_keys = ka_all.shape[1]
    tk = KEY_TILE
    lp = -(-n_keys // tk) * tk
    nt = lp // tk
    n_sel = min(TOPK_MAX, n_keys // 4)

    tq = min(Q_TILE_A, t)
    nq = t // tq
    qi_t = qi.reshape(b, nq, tq, N_IDX_HEADS, IDX_DIM).transpose(0, 1, 4, 3, 2).reshape(
        b, nq, IDX_DIM, N_IDX_HEADS * tq)
    w_t = wi.reshape(b, nq, tq, N_IDX_HEADS).transpose(0, 1, 3, 2)
    qa_t = qa.reshape(b, nq, tq, N_KV_A, GROUP_A, HEAD_DIM).transpose(0, 1, 3, 5, 4, 2).reshape(
        b, nq, N_KV_A, HEAD_DIM, GROUP_A * tq)
    ki_p = _pad_keys(ki_all, lp)
    ka_p = _pad_keys(ka_all, lp).transpose(0, 2, 1, 3)
    va_t = _pad_keys(va_all, lp).reshape(b, nt, tk, N_KV_A, HEAD_DIM).transpose(0, 3, 1, 4, 2)
    oa_t = _dsa(qi_t, w_t, qa_t, ki_p, ka_p, va_t, pos0=pos0, n_keys=n_keys, n_sel=n_sel, tq=tq, tk=tk)
    o_a = oa_t.reshape(b, nq, N_KV_A, HEAD_DIM, GROUP_A, tq).transpose(0, 1, 5, 2, 4, 3).reshape(
        n, WIDTH_A).astype(BF16)

    tqb = min(Q_TILE_B, t)
    qb_t = qb.reshape(b, t, N_HEADS_B, HEAD_DIM).transpose(0, 2, 3, 1)
    kb_p = _pad_keys(kb_all, lp).transpose(0, 2, 1, 3)
    vb_t = _pad_keys(vb_all, lp).reshape(b, nt, tk, N_HEADS_B, HEAD_DIM).transpose(0, 3, 1, 4, 2)
    ob_t = _stick(qb_t, kb_p, vb_t, pos0=pos0, tq=tqb, tk=tk)
    o_b = ob_t.transpose(0, 3, 1, 2).reshape(n, WIDTH_B).astype(BF16)

    y = _tail(x2d, o_a, o_b, ga, gb, p.reshape(n, p.shape[-1]), proj_a, proj_b, w_out, norm_ffn,
              w_up, w_down, norm_ple, w_ple, w_gate)
    return y.reshape(b, t, d_model), new_state


def kernel(x_prompt, x_sample, cache_a_k, cache_a_v, cache_a_kidx, cache_b_k, cache_b_v, p_prompt, p_sample, norm_mix, w_in, g_qa, g_ka, proj_a, proj_b, w_out, norm_ffn, w_up, w_down, norm_ple, w_ple, w_ple_gate):
    depth = w_in.shape[0]
    past = cache_a_k.shape[2]
    d_model = x_prompt.shape[-1]
    split = _SEG["kiwi"][0] + IDX_DIM + N_IDX_HEADS
    assert w_in.shape[2] == split + 3 * WIDTH_B + 2 * d_model
    xp, xs = x_prompt, x_sample
    outs_p, outs_s = [], []
    for i in range(depth):
        w_in_p = jnp.concatenate(
            [w_in[i, :, :split], jnp.zeros((d_model, LANES - IDX_DIM - N_IDX_HEADS), w_in.dtype),
             w_in[i, :, split:]], axis=1).astype(BF16)
        wts = (norm_mix[i], w_in_p, g_qa[i], g_ka[i], proj_a[i].astype(BF16), proj_b[i].astype(BF16),
               w_out[i].astype(BF16), norm_ffn[i], w_up[i].astype(BF16), w_down[i].astype(BF16),
               norm_ple[i], w_ple[i].astype(BF16), w_ple_gate[i].astype(BF16))
        xp, st = _layer(xp, p_prompt[i], 0, None, wts)
        outs_p.append(st)
        caches = (cache_a_k[i], cache_a_v[i], cache_a_kidx[i], cache_b_k[i], cache_b_v[i])
        xs, st = _layer(xs, p_sample[i], past, caches, wts)
        outs_s.append(st)
    stack = lambda outs, k: jnp.stack([o[k] for o in outs], 0)
    return ((xp, xs) + tuple(stack(outs_p, k) for k in range(5))
            + tuple(stack(outs_s, k) for k in range(5)))
```

```python
import functools

import numpy as np
import jax
import jax.numpy as jnp
from jax import lax
from jax.experimental import pallas as pl
from jax.experimental.pallas import tpu as pltpu

CHUNK = 64
HEAD_DIM = 64
N_HEADS_A = 8
N_KV_A = 2
GROUP_A = N_HEADS_A // N_KV_A
N_HEADS_B = 8
N_IDX_HEADS = 8
IDX_DIM = 32
TOPK_MAX = 256
ROPE_THETA = 500000.0
EPS = 1e-6
WIDTH_A = N_HEADS_A * HEAD_DIM
WIDTH_B = N_HEADS_B * HEAD_DIM
KV_A = N_KV_A * HEAD_DIM

LANES = 128
SUBLANES = 8
VMEM_LIMIT_BYTES = 56 * 1024 * 1024

TOKEN_TILE = 256
KEY_TILE = 256
Q_TILE_A = 128
Q_TILE_B = 256
SUPER = 4
COUNT_ROWS = 64
V_ROWS = HEAD_DIM + 16
LOG2E = 1.4426950408889634
SAFE_EXP2 = 60.0

INT_MIN = -2147483648
MASKED = -1e30
STICK_CUTOFF = -120.0
STICK_BOUND_SLACK = 1.01

F32 = jnp.float32
BF16 = jnp.bfloat16
I32 = jnp.int32


def _const_spec(shape):
    nd = len(shape)
    return pl.BlockSpec(shape, lambda *_: (0,) * nd, pipeline_mode=pl.Buffered(1))


def _dot(a, b):
    return jnp.dot(a, b, preferred_element_type=F32)


def _dot_hilo(x, m):
    hi = x.astype(BF16)
    lo = (x - hi.astype(F32)).astype(BF16)
    return _dot(hi, m) + _dot(lo, m)


_SEG = {}
_off = 0
for _name, _w in (("qa", WIDTH_A), ("ka", KV_A), ("va", KV_A), ("qi", N_IDX_HEADS * IDX_DIM),
                  ("kiwi", LANES), ("qb", WIDTH_B), ("kb", WIDTH_B), ("vb", WIDTH_B),
                  ("ga", None), ("gb", None)):
    _SEG[_name] = (_off, _w)
    _off += _w if _w is not None else 0
del _off, _name, _w


def _rope(y, c, s1, s2, half):
    cols = []
    for k in range(y.shape[1] // LANES):
        yk = y[:, k * LANES:(k + 1) * LANES]
        cols.append(yk * c + pltpu.roll(yk, LANES - half, 1) * s1 + pltpu.roll(yk, half, 1) * s2)
    return cols[0] if len(cols) == 1 else jnp.concatenate(cols, axis=1)


def _proj_kernel(x_ref, g_ref, w_ref, gq_ref, gk_ref, gm_ref, tab_ref,
                 qa_ref, ka_ref, va_ref, qi_ref, kiwi_ref, qb_ref, kb_ref, vb_ref, ga_ref, gb_ref,
                 *, d_model):
    x = x_ref[...]
    ms = jnp.mean(x * x, axis=-1, keepdims=True)
    h = ((x * lax.rsqrt(ms + EPS)) * g_ref[...]).astype(BF16)

    def seg(name, width=None):
        off, w = _SEG[name]
        w = width if w is None else w
        return _dot(h, w_ref[:, off:off + w])

    def tab(k):
        return tab_ref[:, k * LANES:(k + 1) * LANES]

    def head_norm(u, gain):
        w = u.shape[1]
        msq = _dot_hilo(u * u, gm_ref[:w, :w])
        return (u * lax.rsqrt(msq + EPS)) * gain

    qa = head_norm(seg("qa"), gq_ref[...])
    qa = _rope(qa, tab(0), tab(1), tab(2), HEAD_DIM // 8)
    qa_ref[...] = (qa * (HEAD_DIM ** -0.5 * LOG2E)).astype(BF16)
    ka = head_norm(seg("ka"), gk_ref[...])
    ka_ref[...] = _rope(ka, tab(0), tab(1), tab(2), HEAD_DIM // 8)
    va_ref[...] = seg("va")
    qi = _rope(seg("qi"), tab(3), tab(4), tab(5), IDX_DIM // 8)
    qi_ref[...] = qi.astype(BF16)
    kiwi_ref[...] = _rope(seg("kiwi"), tab(6), tab(7), tab(8), IDX_DIM // 8)
    qb_ref[...] = (seg("qb") * (HEAD_DIM ** -0.5)).astype(BF16)
    kb_ref[...] = seg("kb")
    vb_ref[...] = seg("vb")
    ga_off = _SEG["ga"][0]
    ga_ref[...] = _dot(h, w_ref[:, ga_off:ga_off + d_model])
    gb_ref[...] = _dot(h, w_ref[:, ga_off + d_model:ga_off + 2 * d_model])


def _rope_tables(pos):
    def one(d):
        r = d // 4
        hr = r // 2
        inv = jnp.power(jnp.float32(ROPE_THETA), -jnp.arange(hr, dtype=F32) * (2.0 / r))
        ang = pos.astype(F32)[:, None] * inv[None, :]
        cos, sin = jnp.cos(ang), jnp.sin(ang)
        t = pos.shape[0]
        one_ = jnp.ones((t, d - r), F32)
        zero = jnp.zeros((t, d - r), F32)
        zh = jnp.zeros((t, hr), F32)
        c = jnp.concatenate([cos, cos, one_], axis=1)
        s1 = jnp.concatenate([-sin, zh, zero], axis=1)
        s2 = jnp.concatenate([zh, sin, zero], axis=1)
        return c, s1, s2

    c64, s164, s264 = one(HEAD_DIM)
    c32, s132, s232 = one(IDX_DIM)
    t = pos.shape[0]
    rep = lambda a, d: jnp.tile(a, (1, LANES // d))
    padc = jnp.ones((t, LANES - IDX_DIM), F32)
    padz = jnp.zeros((t, LANES - IDX_DIM), F32)
    return jnp.concatenate([
        rep(c64, HEAD_DIM), rep(s164, HEAD_DIM), rep(s264, HEAD_DIM),
        rep(c32, IDX_DIM), rep(s132, IDX_DIM), rep(s232, IDX_DIM),
        jnp.concatenate([c32, padc], axis=1), jnp.concatenate([s132, padz], axis=1),
        jnp.concatenate([s232, padz], axis=1)], axis=1)


def _project(x2d, seq_len, pos, norm_mix, w_in_p, g_qa, g_ka):
    n, d_model = x2d.shape
    tm = min(TOKEN_TILE, n)
    assert n % tm == 0 and (seq_len % tm == 0 or tm % seq_len == 0)
    tables = _rope_tables(pos)
    if seq_len < tm:
        tables = jnp.tile(tables, (tm // seq_len, 1))
    tiles_per_seq = max(seq_len // tm, 1)
    gm = jnp.asarray(np.kron(np.eye(N_HEADS_A, dtype=np.float32),
                             np.full((HEAD_DIM, HEAD_DIM), 1.0 / HEAD_DIM, np.float32)), BF16)
    gq = jnp.tile(g_qa.reshape(1, HEAD_DIM), (1, N_HEADS_A))
    gk = jnp.tile(g_ka.reshape(1, HEAD_DIM), (1, N_KV_A))
    wcols = w_in_p.shape[1]

    row = lambda w: pl.BlockSpec((tm, w), lambda i: (i, 0))
    out_shapes = [
        jax.ShapeDtypeStruct((n, WIDTH_A), BF16), jax.ShapeDtypeStruct((n, KV_A), F32),
        jax.ShapeDtypeStruct((n, KV_A), F32), jax.ShapeDtypeStruct((n, N_IDX_HEADS * IDX_DIM), BF16),
        jax.ShapeDtypeStruct((n, LANES), F32), jax.ShapeDtypeStruct((n, WIDTH_B), BF16),
        jax.ShapeDtypeStruct((n, WIDTH_B), F32), jax.ShapeDtypeStruct((n, WIDTH_B), F32),
        jax.ShapeDtypeStruct((n, d_model), F32), jax.ShapeDtypeStruct((n, d_model), F32)]
    return pl.pallas_call(
        functools.partial(_proj_kernel, d_model=d_model),
        grid=(n // tm,),
        in_specs=[row(d_model), _const_spec((1, d_model)), _const_spec((d_model, wcols)),
                  _const_spec((1, WIDTH_A)), _const_spec((1, KV_A)), _const_spec((WIDTH_A, WIDTH_A)),
                  pl.BlockSpec((tm, 9 * LANES), lambda i: (i % tiles_per_seq, 0))],
        out_specs=[row(s.shape[1]) for s in out_shapes],
        out_shape=out_shapes,
        compiler_params=pltpu.CompilerParams(dimension_semantics=("arbitrary",),
                                             vmem_limit_bytes=VMEM_LIMIT_BYTES),
        name="proj",
    )(x2d, norm_mix.reshape(1, d_model), w_in_p, gq, gk, gm, tables)


def _max_row_norm(k_ref, tk, out_shape):
    def body(j, mx):
        kf = k_ref[pl.ds(pl.multiple_of(j * tk, tk), tk), :].astype(F32)
        sq = jnp.sum(kf * kf, axis=1, keepdims=True)
        return jnp.maximum(mx, jnp.max(sq, axis=0, keepdims=True))
    mx = lax.fori_loop(0, k_ref.shape[0] // tk, body, jnp.zeros((1, 1), F32))
    return jnp.sqrt(jnp.broadcast_to(mx, out_shape))


def _dsa_kernel(qi_ref, w_ref, qa_ref, ki_ref, ka_ref, va_ref, o_ref,
                sc_ref, acc_ref, kmax_ref, *, pos0, n_keys, n_sel, tq, tk, idx_bits):
    i = pl.program_id(1)
    pos_first = pos0 + i * tq
    lim_max = jnp.minimum(((pos_first + tq - 1) // CHUNK + 1) * CHUNK, n_keys)
    n_tiles = (lim_max + tk - 1) // tk
    lane_pos = pos_first + lax.broadcasted_iota(I32, (1, tq), 1)
    lim_row = jnp.minimum((lax.shift_right_arithmetic(lane_pos, int(np.log2(CHUNK))) + 1) * CHUNK, n_keys)
    neg_inf = jnp.float32(-jnp.inf)

    n_super = (n_tiles + SUPER - 1) // SUPER
    row_iota = lax.broadcasted_iota(I32, (tk, tq), 0)

    qi = qi_ref[...]
    w = w_ref[...]

    def score_tiles(js, _):
        for c in range(SUPER):
            r0 = pl.multiple_of((js * SUPER + c) * tk, tk)
            lg = _dot(ki_ref[pl.ds(r0, tk), :], qi)
            sc = jnp.zeros((tk, tq), F32)
            for h in range(N_IDX_HEADS):
                sc = sc + jnp.maximum(lg[:, h * tq:(h + 1) * tq], 0.0) * w[h:h + 1, :]
            sc_ref[pl.ds(r0, tk), :] = jnp.where(row_iota + r0 < lim_row, sc, neg_inf)
        return 0

    lax.fori_loop(0, n_super, score_tiles, 0)

    n_steps = (n_tiles + 1) // 2
    tk2 = 2 * tk
    row_iota2 = lax.broadcasted_iota(I32, (tk2, tq), 0)

    def count(pred):
        def body(j, acc):
            r0 = pl.multiple_of(j * tk2, tk2)
            hit = jnp.where(pred(sc_ref[pl.ds(r0, tk2), :], r0), 1, 0).astype(I32)
            for r in range(tk2 // COUNT_ROWS):
                acc = acc + hit[r * COUNT_ROWS:(r + 1) * COUNT_ROWS]
            return acc
        acc = lax.fori_loop(0, n_steps, body, jnp.zeros((COUNT_ROWS, tq), I32))
        return jnp.sum(acc, axis=0, keepdims=True)

    def from_key(k):
        return pltpu.bitcast(k ^ (lax.shift_right_arithmetic(k, 31) & 0x7FFFFFFF), F32)

    def bisect(t, thr):
        cand = thr ^ lax.shift_left(jnp.int32(1), 31 - t)
        cand_f = from_key(cand)
        cnt = count(lambda s, r0: s >= cand_f)
        return jnp.where(cnt >= n_sel, cand, thr)

    thr_key = lax.fori_loop(0, 32, bisect, jnp.full((1, tq), INT_MIN, I32))
    found = thr_key > INT_MIN
    thr = jnp.where(found, from_key(thr_key), neg_inf)
    c_ge = count(lambda s, r0: s >= thr)
    c_gt = count(lambda s, r0: s > thr)
    need = n_sel - c_gt
    tie = jnp.where(c_ge - c_gt > need, jnp.where(found, 1, 0), 0).astype(I32)

    def tie_cut():
        def step(t, c0):
            cand = c0 | lax.shift_left(jnp.int32(1), idx_bits - 1 - t)
            cnt = count(lambda s, r0: jnp.where(s == thr, row_iota2 + r0, cand) < cand)
            return jnp.where(cnt < need, cand, c0)
        return lax.fori_loop(0, idx_bits, step, jnp.zeros((1, tq), I32))

    any_tie = jnp.max(tie) > 0
    cut = lax.cond(any_tie, tie_cut, lambda: jnp.zeros((1, tq), I32))
    cut = jnp.where(found, jnp.where(tie > 0, cut, jnp.int32(2 ** 30)), -1)

    acc_ref[...] = jnp.zeros(acc_ref.shape, F32)

    def masked_scores(j):
        r0 = pl.multiple_of(j * tk, tk)
        sc = sc_ref[pl.ds(r0, tk), :]
        bias = jnp.where(sc > thr, 0.0,
                         jnp.where(sc == thr, jnp.where(row_iota + r0 <= cut, 0.0, MASKED), MASKED))
        bias = jnp.concatenate([bias] * GROUP_A, axis=1)
        return [_dot(ka_ref[g, pl.ds(r0, tk), :], qa_ref[g]) + bias for g in range(N_KV_A)]

    def attend_bounded():
        def tiles(js, _):
            pv = [0.0] * N_KV_A
            for c in range(SUPER):
                j = js * SUPER + c
                for g, s in enumerate(masked_scores(j)):
                    pv[g] = pv[g] + _dot(va_ref[g, j], jnp.exp2(s).astype(BF16))
            for g in range(N_KV_A):
                acc_ref[g] += pv[g]
            return 0
        lax.fori_loop(0, n_super, tiles, 0)

    def attend_online():
        def tile(j, ms):
            new_ms = []
            for g, s in enumerate(masked_scores(j)):
                m_new = jnp.maximum(ms[g], jnp.max(s, axis=0, keepdims=True))
                p = jnp.exp2(s - m_new).astype(BF16)
                acc_ref[g] = acc_ref[g] * jnp.exp2(ms[g] - m_new) + _dot(va_ref[g, j], p)
                new_ms.append(m_new)
            return tuple(new_ms)
        m0 = jnp.full((1, GROUP_A * tq), MASKED, F32)
        lax.fori_loop(0, n_tiles, tile, (m0,) * N_KV_A)

    @pl.when(i == 0)
    def _():
        for g in range(N_KV_A):
            kmax_ref[g] = _max_row_norm(ka_ref.at[g], tk, kmax_ref.shape[1:])

    score_bound = jnp.float32(0.0)
    for g in range(N_KV_A):
        qf = qa_ref[g].astype(F32)
        qn = jnp.sqrt(jnp.sum(qf * qf, axis=0, keepdims=True))
        score_bound = jnp.maximum(score_bound, jnp.max(qn * kmax_ref[g, 0:1, 0:1]))
    lax.cond(score_bound <= SAFE_EXP2, attend_bounded, attend_online)
    for g in range(N_KV_A):
        o_ref[g] = acc_ref[g, :HEAD_DIM] / acc_ref[g, HEAD_DIM:HEAD_DIM + 1]


def _dsa(qi_t, w_t, qa_t, ki, ka, va_t, *, pos0, n_keys, n_sel, tq, tk):
    b, nq = qi_t.shape[:2]
    lp = ki.shape[1]
    nt = lp // tk
    sq = None
    kern = functools.partial(_dsa_kernel, pos0=pos0, n_keys=n_keys, n_sel=n_sel, tq=tq, tk=tk,
                             idx_bits=int(np.ceil(np.log2(lp))))
    per_batch = lambda shape: pl.BlockSpec((sq,) + shape, lambda bi, i: (bi,) + (0,) * len(shape),
                                           pipeline_mode=pl.Buffered(1))
    per_q = lambda shape: pl.BlockSpec((sq, sq) + shape, lambda bi, i: (bi, i) + (0,) * len(shape))
    return pl.pallas_call(
        kern,
        grid=(b, nq),
        in_specs=[per_q((IDX_DIM, N_IDX_HEADS * tq)), per_q((N_IDX_HEADS, tq)),
                  per_q((N_KV_A, HEAD_DIM, GROUP_A * tq)),
                  per_batch((lp, IDX_DIM)), per_batch((N_KV_A, lp, HEAD_DIM)),
                  per_batch((N_KV_A, nt, V_ROWS, tk))],
        out_specs=per_q((N_KV_A, HEAD_DIM, GROUP_A * tq)),
        out_shape=jax.ShapeDtypeStruct((b, nq, N_KV_A, HEAD_DIM, GROUP_A * tq), F32),
        scratch_shapes=[pltpu.VMEM((lp, tq), F32),
                        pltpu.VMEM((N_KV_A, V_ROWS, GROUP_A * tq), F32),
                        pltpu.VMEM((N_KV_A, SUBLANES, LANES), F32)],
        compiler_params=pltpu.CompilerParams(dimension_semantics=("arbitrary", "arbitrary"),
                                             vmem_limit_bytes=VMEM_LIMIT_BYTES),
        name="dsa",
    )(qi_t, w_t, qa_t, ki, ka, va_t)


def _stick_kernel(q_ref, k_ref, v_ref, tri_ref, o_ref, kmax_ref, *, pos0, tq, tk):
    i = pl.program_id(2)

    @pl.when(i == 0)
    def _():
        kmax_ref[...] = _max_row_norm(k_ref, tk, kmax_ref.shape)

    q = q_ref[...]
    qf = q.astype(F32)
    qn = jnp.sqrt(jnp.sum(qf * qf, axis=0, keepdims=True))
    bound = qn * kmax_ref[0:1, 0:1] * STICK_BOUND_SLACK
    pos_first = pos0 + i * tq
    pos_row = pos_first + lax.broadcasted_iota(I32, (1, tq), 1)
    row_iota = lax.broadcasted_iota(I32, (tk, tq), 0)
    j_first = jnp.maximum(pos_first + tq - 2, 0) // tk
    tri = tri_ref[...]

    def cond(state):
        j, live, _, _ = state
        return jnp.logical_and(j >= 0, live > 0)

    def body(state):
        j, _, carry, acc = state
        r0 = pl.multiple_of(j * tk, tk)
        z = _dot(k_ref[pl.ds(r0, tk), :], q)
        vis = row_iota + r0 < pos_row
        softplus = jnp.maximum(z, 0.0) + jnp.log(1.0 + jnp.exp(-jnp.abs(z)))
        log_keep = jnp.where(vis, -softplus, 0.0)
        incl = _dot_hilo_left(tri, log_keep)
        a = jnp.where(vis, jnp.exp(z + incl + carry), 0.0)
        acc = acc + _dot(v_ref[j], a.astype(BF16))
        carry = carry + incl[0:1, :]
        live = (jnp.max(bound + carry) >= STICK_CUTOFF).astype(I32)
        return j - 1, live, carry, acc

    init = (j_first, jnp.int32(1), jnp.zeros((1, tq), F32), jnp.zeros((HEAD_DIM, tq), F32))
    o_ref[...] = lax.while_loop(cond, body, init)[3]


def _dot_hilo_left(m, x):
    hi = x.astype(BF16)
    lo = (x - hi.astype(F32)).astype(BF16)
    return _dot(m, hi) + _dot(m, lo)


def _stick(q_t, k, v_t, *, pos0, tq, tk):
    b, nh, _, t = q_t.shape
    lp = k.shape[2]
    nt = lp // tk
    sq = None
    tri = jnp.asarray(np.triu(np.ones((tk, tk), np.float32)), BF16)
    kern = functools.partial(_stick_kernel, pos0=pos0, tq=tq, tk=tk)
    return pl.pallas_call(
        kern,
        grid=(b, nh, t // tq),
        in_specs=[pl.BlockSpec((sq, sq, HEAD_DIM, tq), lambda bi, h, i: (bi, h, 0, i)),
                  pl.BlockSpec((sq, sq, lp, HEAD_DIM), lambda bi, h, i: (bi, h, 0, 0)),
                  pl.BlockSpec((sq, sq, nt, HEAD_DIM, tk), lambda bi, h, i: (bi, h, 0, 0, 0)),
                  _const_spec((tk, tk))],
        out_specs=pl.BlockSpec((sq, sq, HEAD_DIM, tq), lambda bi, h, i: (bi, h, 0, i)),
        out_shape=jax.ShapeDtypeStruct((b, nh, HEAD_DIM, t), F32),
        scratch_shapes=[pltpu.VMEM((SUBLANES, LANES), F32)],
        compiler_params=pltpu.CompilerParams(dimension_semantics=("arbitrary",) * 3,
                                             vmem_limit_bytes=VMEM_LIMIT_BYTES),
        name="stick",
    )(q_t, k, v_t, tri)


def _rms(x, g):
    ms = jnp.mean(x * x, axis=-1, keepdims=True)
    return (x * lax.rsqrt(ms + EPS)) * g


def _tail_kernel(x_ref, oa_ref, ob_ref, ga_ref, gb_ref, p_ref, pa_ref, pb_ref, wo_ref,
                 nf_ref, wu_ref, wd_ref, np_ref, wp_ref, wg_ref, y_ref, *, ff_chunk):
    mix = (jax.nn.sigmoid(ga_ref[...]) * _dot(oa_ref[...], pa_ref[...])
           + jax.nn.sigmoid(gb_ref[...]) * _dot(ob_ref[...], pb_ref[...]))
    x = x_ref[...] + _dot(mix.astype(BF16), wo_ref[...])
    h = _rms(x, nf_ref[...]).astype(BF16)
    d_ff = wu_ref.shape[1]
    ffn = jnp.zeros_like(x)
    for c in range(d_ff // ff_chunk):
        u = jnp.maximum(_dot(h, wu_ref[:, c * ff_chunk:(c + 1) * ff_chunk]), 0.0)
        ffn = ffn + _dot((u * u).astype(BF16), wd_ref[c * ff_chunk:(c + 1) * ff_chunk, :])
    x = x + ffn
    h = _rms(x, np_ref[...]).astype(BF16)
    y_ref[...] = x + _dot(p_ref[...].astype(BF16), wp_ref[...]) * jax.nn.sigmoid(_dot(h, wg_ref[...]))


def _tail(x2d, o_a, o_b, ga, gb, p2d, proj_a, proj_b, w_out, norm_ffn, w_up, w_down,
          norm_ple, w_ple, w_gate):
    n, d_model = x2d.shape
    tm = min(TOKEN_TILE, n)
    assert n % tm == 0
    d_ff = w_up.shape[1]
    row = lambda w: pl.BlockSpec((tm, w), lambda i: (i, 0))
    weights = (proj_a, proj_b, w_out, norm_ffn.reshape(1, d_model), w_up, w_down,
               norm_ple.reshape(1, d_model), w_ple, w_gate)
    return pl.pallas_call(
        functools.partial(_tail_kernel, ff_chunk=min(1024, d_ff)),
        grid=(n // tm,),
        in_specs=[row(d_model), row(WIDTH_A), row(WIDTH_B), row(d_model), row(d_model),
                  row(p2d.shape[1])] + [_const_spec(w.shape) for w in weights],
        out_specs=row(d_model),
        out_shape=jax.ShapeDtypeStruct((n, d_model), F32),
        compiler_params=pltpu.CompilerParams(dimension_semantics=("arbitrary",),
                                             vmem_limit_bytes=VMEM_LIMIT_BYTES),
        name="tail",
    )(x2d, o_a, o_b, ga, gb, p2d, *weights)


def _pad_keys(a, lp):
    return jnp.pad(a, ((0, 0), (0, lp - a.shape[1])) + ((0, 0),) * (a.ndim - 2))


def _layer(x, p, pos0, caches, wts):
    (norm_mix, w_in_p, g_qa, g_ka, proj_a, proj_b, w_out, norm_ffn, w_up, w_down,
     norm_ple, w_ple, w_gate) = wts
    b, t, d_model = x.shape
    n = b * t
    pos = pos0 + jnp.arange(t, dtype=I32)
    x2d = x.reshape(n, d_model)
    qa, ka, va, qi, kiwi, qb, kb, vb, ga, gb = _project(x2d, t, pos, norm_mix, w_in_p, g_qa, g_ka)

    ka = ka.reshape(b, t, N_KV_A, HEAD_DIM)
    va = va.reshape(b, t, N_KV_A, HEAD_DIM)
    ki = kiwi[:, :IDX_DIM].reshape(b, t, IDX_DIM)
    kb = kb.reshape(b, t, N_HEADS_B, HEAD_DIM)
    vb = vb.reshape(b, t, N_HEADS_B, HEAD_DIM)
    new_state = (ka, va, ki, kb, vb)
    wi = kiwi[:, IDX_DIM:IDX_DIM + N_IDX_HEADS] * ((N_IDX_HEADS ** -0.5) * (IDX_DIM ** -0.5))

    if caches is not None:
        ka_all, va_all, ki_all, kb_all, vb_all = (
            jnp.concatenate([c.astype(BF16), s.astype(BF16)], axis=1) for c, s in zip(caches, new_state))
    else:
        ka_all, va_all, ki_all, kb_all, vb_all = (s.astype(BF16) for s in new_state)
    n_keys = ka_all.shape[1]
    tk = KEY_TILE
    lp = -(-n_keys // (SUPER * tk)) * (SUPER * tk)
    nt = lp // tk
    n_sel = min(TOPK_MAX, n_keys // 4)

    tq = min(Q_TILE_A, t)
    nq = t // tq
    qi_t = qi.reshape(b, nq, tq, N_IDX_HEADS, IDX_DIM).transpose(0, 1, 4, 3, 2).reshape(
        b, nq, IDX_DIM, N_IDX_HEADS * tq)
    w_t = wi.reshape(b, nq, tq, N_IDX_HEADS).transpose(0, 1, 3, 2)
    qa_t = qa.reshape(b, nq, tq, N_KV_A, GROUP_A, HEAD_DIM).transpose(0, 1, 3, 5, 4, 2).reshape(
        b, nq, N_KV_A, HEAD_DIM, GROUP_A * tq)
    ki_p = _pad_keys(ki_all, lp)
    ka_p = _pad_keys(ka_all, lp).transpose(0, 2, 1, 3)
    va_t = _pad_keys(va_all, lp).reshape(b, nt, tk, N_KV_A, HEAD_DIM).transpose(0, 3, 1, 4, 2)
    va_t = jnp.concatenate([va_t, jnp.ones((b, N_KV_A, nt, V_ROWS - HEAD_DIM, tk), BF16)], axis=3)
    oa_t = _dsa(qi_t, w_t, qa_t, ki_p, ka_p, va_t, pos0=pos0, n_keys=n_keys, n_sel=n_sel, tq=tq, tk=tk)
    o_a = oa_t.reshape(b, nq, N_KV_A, HEAD_DIM, GROUP_A, tq).transpose(0, 1, 5, 2, 4, 3).reshape(
        n, WIDTH_A).astype(BF16)

    tqb = min(Q_TILE_B, t)
    qb_t = qb.reshape(b, t, N_HEADS_B, HEAD_DIM).transpose(0, 2, 3, 1)
    kb_p = _pad_keys(kb_all, lp).transpose(0, 2, 1, 3)
    vb_t = _pad_keys(vb_all, lp).reshape(b, nt, tk, N_HEADS_B, HEAD_DIM).transpose(0, 3, 1, 4, 2)
    ob_t = _stick(qb_t, kb_p, vb_t, pos0=pos0, tq=tqb, tk=tk)
    o_b = ob_t.transpose(0, 3, 1, 2).reshape(n, WIDTH_B).astype(BF16)

    y = _tail(x2d, o_a, o_b, ga, gb, p.reshape(n, p.shape[-1]), proj_a, proj_b, w_out, norm_ffn,
              w_up, w_down, norm_ple, w_ple, w_gate)
    return y.reshape(b, t, d_model), new_state


def kernel(x_prompt, x_sample, cache_a_k, cache_a_v, cache_a_kidx, cache_b_k, cache_b_v, p_prompt, p_sample, norm_mix, w_in, g_qa, g_ka, proj_a, proj_b, w_out, norm_ffn, w_up, w_down, norm_ple, w_ple, w_ple_gate):
    depth = w_in.shape[0]
    past = cache_a_k.shape[2]
    d_model = x_prompt.shape[-1]
    split = _SEG["kiwi"][0] + IDX_DIM + N_IDX_HEADS
    assert w_in.shape[2] == split + 3 * WIDTH_B + 2 * d_model
    xp, xs = x_prompt, x_sample
    outs_p, outs_s = [], []
    for i in range(depth):
        w_in_p = jnp.concatenate(
            [w_in[i, :, :split], jnp.zeros((d_model, LANES - IDX_DIM - N_IDX_HEADS), w_in.dtype),
             w_in[i, :, split:]], axis=1).astype(BF16)
        wts = (norm_mix[i], w_in_p, g_qa[i], g_ka[i], proj_a[i].astype(BF16), proj_b[i].astype(BF16),
               w_out[i].astype(BF16), norm_ffn[i], w_up[i].astype(BF16), w_down[i].astype(BF16),
               norm_ple[i], w_ple[i].astype(BF16), w_ple_gate[i].astype(BF16))
        xp, st = _layer(xp, p_prompt[i], 0, None, wts)
        outs_p.append(st)
        caches = (cache_a_k[i], cache_a_v[i], cache_a_kidx[i], cache_b_k[i], cache_b_v[i])
        xs, st = _layer(xs, p_sample[i], past, caches, wts)
        outs_s.append(st)
    stack = lambda outs, k: jnp.stack([o[k] for o in outs], 0)
    return ((xp, xs) + tuple(stack(outs_p, k) for k in range(5))
            + tuple(stack(outs_s, k) for k in range(5)))
```

```python
import functools

import numpy as np
import jax
import jax.numpy as jnp
from jax import lax
from jax.experimental import pallas as pl
from jax.experimental.pallas import tpu as pltpu

CHUNK = 64
HEAD_DIM = 64
N_HEADS_A = 8
N_KV_A = 2
GROUP_A = N_HEADS_A // N_KV_A
N_HEADS_B = 8
N_IDX_HEADS = 8
IDX_DIM = 32
TOPK_MAX = 256
ROPE_THETA = 500000.0
EPS = 1e-6
WIDTH_A = N_HEADS_A * HEAD_DIM
WIDTH_B = N_HEADS_B * HEAD_DIM
KV_A = N_KV_A * HEAD_DIM

LANES = 128
SUBLANES = 8
VMEM_LIMIT_BYTES = 56 * 1024 * 1024

TOKEN_TILE = 256
KEY_TILE = 256
Q_TILE_A = 128
Q_TILE_B = 256
SUPER = 4
COUNT_ROWS = 64
V_ROWS = HEAD_DIM + 16
LOG2E = 1.4426950408889634
SAFE_EXP2 = 60.0
NORM_SLACK = 1.001
SEARCH_CAP = 160
NARROW_BRACKET = 48

MASKED = -1e30
STICK_CUTOFF = -120.0 * LOG2E
STICK_BOUND_SLACK = 1.01
STICK_HEADS = 4

F32 = jnp.float32
BF16 = jnp.bfloat16
I32 = jnp.int32


def _const_spec(shape):
    nd = len(shape)
    return pl.BlockSpec(shape, lambda *_: (0,) * nd, pipeline_mode=pl.Buffered(1))


def _dot(a, b):
    return jnp.dot(a, b, preferred_element_type=F32)


def _dot_hilo(x, m):
    hi = x.astype(BF16)
    lo = (x - hi.astype(F32)).astype(BF16)
    return _dot(hi, m) + _dot(lo, m)


_SEG = {}
_off = 0
for _name, _w in (("qa", WIDTH_A), ("ka", KV_A), ("va", KV_A), ("qi", N_IDX_HEADS * IDX_DIM),
                  ("kiwi", LANES), ("qb", WIDTH_B), ("kb", WIDTH_B), ("vb", WIDTH_B),
                  ("ga", None), ("gb", None)):
    _SEG[_name] = (_off, _w)
    _off += _w if _w is not None else 0
del _off, _name, _w


def _rope(y, c, s1, s2, half):
    cols = []
    for k in range(y.shape[1] // LANES):
        yk = y[:, k * LANES:(k + 1) * LANES]
        cols.append(yk * c + pltpu.roll(yk, LANES - half, 1) * s1 + pltpu.roll(yk, half, 1) * s2)
    return cols[0] if len(cols) == 1 else jnp.concatenate(cols, axis=1)


def _proj_kernel(x_ref, g_ref, w_ref, gq_ref, gk_ref, gm_ref, tab_ref,
                 qa_ref, ka_ref, va_ref, qi_ref, kiwi_ref, qb_ref, kb_ref, vb_ref, ga_ref, gb_ref,
                 *, d_model):
    x = x_ref[...]
    ms = jnp.mean(x * x, axis=-1, keepdims=True)
    h = ((x * lax.rsqrt(ms + EPS)) * g_ref[...]).astype(BF16)

    def seg(name, width=None):
        off, w = _SEG[name]
        w = width if w is None else w
        return _dot(h, w_ref[:, off:off + w])

    def tab(k):
        return tab_ref[:, k * LANES:(k + 1) * LANES]

    def head_norm(u, gain):
        w = u.shape[1]
        msq = _dot_hilo(u * u, gm_ref[:w, :w])
        return (u * lax.rsqrt(msq + EPS)) * gain

    qa = head_norm(seg("qa"), gq_ref[...])
    qa = _rope(qa, tab(0), tab(1), tab(2), HEAD_DIM // 8)
    qa_ref[...] = (qa * (HEAD_DIM ** -0.5 * LOG2E)).astype(BF16)
    ka = head_norm(seg("ka"), gk_ref[...])
    ka_ref[...] = _rope(ka, tab(0), tab(1), tab(2), HEAD_DIM // 8)
    va_ref[...] = seg("va")
    qi = _rope(seg("qi"), tab(3), tab(4), tab(5), IDX_DIM // 8)
    qi_ref[...] = qi.astype(BF16)
    kiwi_ref[...] = _rope(seg("kiwi"), tab(6), tab(7), tab(8), IDX_DIM // 8)
    qb_ref[...] = (seg("qb") * (HEAD_DIM ** -0.5 * LOG2E)).astype(BF16)
    kb_ref[...] = seg("kb")
    vb_ref[...] = seg("vb")
    ga_off = _SEG["ga"][0]
    ga_ref[...] = _dot(h, w_ref[:, ga_off:ga_off + d_model])
    gb_ref[...] = _dot(h, w_ref[:, ga_off + d_model:ga_off + 2 * d_model])


def _rope_tables(pos):
    def one(d):
        r = d // 4
        hr = r // 2
        inv = jnp.power(jnp.float32(ROPE_THETA), -jnp.arange(hr, dtype=F32) * (2.0 / r))
        ang = pos.astype(F32)[:, None] * inv[None, :]
        cos, sin = jnp.cos(ang), jnp.sin(ang)
        t = pos.shape[0]
        one_ = jnp.ones((t, d - r), F32)
        zero = jnp.zeros((t, d - r), F32)
        zh = jnp.zeros((t, hr), F32)
        c = jnp.concatenate([cos, cos, one_], axis=1)
        s1 = jnp.concatenate([-sin, zh, zero], axis=1)
        s2 = jnp.concatenate([zh, sin, zero], axis=1)
        return c, s1, s2

    c64, s164, s264 = one(HEAD_DIM)
    c32, s132, s232 = one(IDX_DIM)
    t = pos.shape[0]
    rep = lambda a, d: jnp.tile(a, (1, LANES // d))
    padc = jnp.ones((t, LANES - IDX_DIM), F32)
    padz = jnp.zeros((t, LANES - IDX_DIM), F32)
    return jnp.concatenate([
        rep(c64, HEAD_DIM), rep(s164, HEAD_DIM), rep(s264, HEAD_DIM),
        rep(c32, IDX_DIM), rep(s132, IDX_DIM), rep(s232, IDX_DIM),
        jnp.concatenate([c32, padc], axis=1), jnp.concatenate([s132, padz], axis=1),
        jnp.concatenate([s232, padz], axis=1)], axis=1)


def _project(x2d, seq_len, pos, norm_mix, w_in_p, g_qa, g_ka):
    n, d_model = x2d.shape
    tm = min(TOKEN_TILE, n)
    assert n % tm == 0 and (seq_len % tm == 0 or tm % seq_len == 0)
    tables = _rope_tables(pos)
    if seq_len < tm:
        tables = jnp.tile(tables, (tm // seq_len, 1))
    tiles_per_seq = max(seq_len // tm, 1)
    gm = jnp.asarray(np.kron(np.eye(N_HEADS_A, dtype=np.float32),
                             np.full((HEAD_DIM, HEAD_DIM), 1.0 / HEAD_DIM, np.float32)), BF16)
    gq = jnp.tile(g_qa.reshape(1, HEAD_DIM), (1, N_HEADS_A))
    gk = jnp.tile(g_ka.reshape(1, HEAD_DIM), (1, N_KV_A))
    wcols = w_in_p.shape[1]

    row = lambda w: pl.BlockSpec((tm, w), lambda i: (i, 0))
    out_shapes = [
        jax.ShapeDtypeStruct((n, WIDTH_A), BF16), jax.ShapeDtypeStruct((n, KV_A), F32),
        jax.ShapeDtypeStruct((n, KV_A), F32), jax.ShapeDtypeStruct((n, N_IDX_HEADS * IDX_DIM), BF16),
        jax.ShapeDtypeStruct((n, LANES), F32), jax.ShapeDtypeStruct((n, WIDTH_B), BF16),
        jax.ShapeDtypeStruct((n, WIDTH_B), F32), jax.ShapeDtypeStruct((n, WIDTH_B), F32),
        jax.ShapeDtypeStruct((n, d_model), F32), jax.ShapeDtypeStruct((n, d_model), F32)]
    return pl.pallas_call(
        functools.partial(_proj_kernel, d_model=d_model),
        grid=(n // tm,),
        in_specs=[row(d_model), _const_spec((1, d_model)), _const_spec((d_model, wcols)),
                  _const_spec((1, WIDTH_A)), _const_spec((1, KV_A)), _const_spec((WIDTH_A, WIDTH_A)),
                  pl.BlockSpec((tm, 9 * LANES), lambda i: (i % tiles_per_seq, 0))],
        out_specs=[row(s.shape[1]) for s in out_shapes],
        out_shape=out_shapes,
        compiler_params=pltpu.CompilerParams(dimension_semantics=("arbitrary",),
                                             vmem_limit_bytes=VMEM_LIMIT_BYTES),
        name="proj",
    )(x2d, norm_mix.reshape(1, d_model), w_in_p, gq, gk, gm, tables)


def _max_row_norm(k_ref, tk, out_shape):
    ones = jnp.ones((k_ref.shape[1], LANES), BF16)

    def body(j, mx):
        kf = k_ref[pl.ds(pl.multiple_of(j * tk, tk), tk), :].astype(F32)
        sq = _dot_hilo(kf * kf, ones)
        part = sq[0:SUBLANES]
        for r in range(1, tk // SUBLANES):
            part = jnp.maximum(part, sq[r * SUBLANES:(r + 1) * SUBLANES])
        return jnp.maximum(mx, part)
    mx = lax.fori_loop(0, k_ref.shape[0] // tk, body, jnp.zeros((SUBLANES, LANES), F32))
    mx = jnp.max(mx, axis=0, keepdims=True) * NORM_SLACK
    return jnp.sqrt(jnp.broadcast_to(mx, out_shape))


def _dsa_kernel(qi_ref, w_ref, qa_ref, ki_ref, ka_ref, va_ref, o_ref,
                sc_ref, acc_ref, kmax_ref, *, pos0, n_keys, n_sel, tq, tk, idx_bits):
    i = pl.program_id(1)
    pos_first = pos0 + i * tq
    lim_max = jnp.minimum(((pos_first + tq - 1) // CHUNK + 1) * CHUNK, n_keys)
    n_tiles = (lim_max + tk - 1) // tk
    lane_pos = pos_first + lax.broadcasted_iota(I32, (1, tq), 1)
    lim_row = jnp.minimum((lax.shift_right_arithmetic(lane_pos, int(np.log2(CHUNK))) + 1) * CHUNK, n_keys)
    neg_inf = jnp.float32(-jnp.inf)

    n_super = (n_tiles + SUPER - 1) // SUPER
    row_iota = lax.broadcasted_iota(I32, (tk, tq), 0)

    qi = qi_ref[...]
    w = w_ref[...]

    def fold_rows(x, op):
        out = x[0:SUBLANES]
        for r in range(1, x.shape[0] // SUBLANES):
            out = op(out, x[r * SUBLANES:(r + 1) * SUBLANES])
        return out

    def score_tiles(js, carry):
        smax, smin = carry
        for c in range(SUPER):
            r0 = pl.multiple_of((js * SUPER + c) * tk, tk)
            lg = _dot(ki_ref[pl.ds(r0, tk), :], qi)
            sc = jnp.zeros((tk, tq), F32)
            for h in range(N_IDX_HEADS):
                sc = sc + jnp.maximum(lg[:, h * tq:(h + 1) * tq], 0.0) * w[h:h + 1, :]
            adm = row_iota + r0 < lim_row
            masked = jnp.where(adm, sc, neg_inf)
            sc_ref[pl.ds(r0, tk), :] = masked
            smax = jnp.maximum(smax, fold_rows(masked, jnp.maximum))
            smin = jnp.minimum(smin, fold_rows(jnp.where(adm, sc, -neg_inf), jnp.minimum))
        return smax, smin

    smax, smin = lax.fori_loop(0, n_super, score_tiles,
                               (jnp.full((SUBLANES, tq), neg_inf, F32), jnp.full((SUBLANES, tq), -neg_inf, F32)))
    smax = jnp.max(smax, axis=0, keepdims=True)
    smin = jnp.min(smin, axis=0, keepdims=True)

    n_steps = (n_tiles + 1) // 2
    tk2 = 2 * tk
    row_iota2 = lax.broadcasted_iota(I32, (tk2, tq), 0)

    def count(*preds):
        def body(j, accs):
            r0 = pl.multiple_of(j * tk2, tk2)
            s = sc_ref[pl.ds(r0, tk2), :]
            out = []
            for pred, acc in zip(preds, accs):
                hit = jnp.where(pred(s, r0), 1, 0).astype(I32)
                for r in range(tk2 // COUNT_ROWS):
                    acc = acc + hit[r * COUNT_ROWS:(r + 1) * COUNT_ROWS]
                out.append(acc)
            return tuple(out)
        accs = lax.fori_loop(0, n_steps, body, (jnp.zeros((COUNT_ROWS, tq), I32),) * len(preds))
        return [jnp.sum(a, axis=0, keepdims=True) for a in accs]

    def to_key(x):
        bits = pltpu.bitcast(x, I32)
        return bits ^ (lax.shift_right_arithmetic(bits, 31) & 0x7FFFFFFF)

    def from_key(k):
        return pltpu.bitcast(k ^ (lax.shift_right_arithmetic(k, 31) & 0x7FFFFFFF), F32)

    k_f = jnp.float32(n_sel)
    n_adm = lim_row
    zero = jnp.zeros((1, tq), F32)
    fge0, fgt0 = count(lambda s, r0: s >= zero, lambda s, r0: s > zero)
    take_all = n_adm <= n_sel
    zero_cut = fgt0 == n_sel
    zero_tie = jnp.logical_and(fgt0 < n_sel, fge0 >= n_sel)
    pos_side = fgt0 > n_sel
    neg_side = fge0 < n_sel
    as_i = lambda m: jnp.where(m, 1, 0).astype(I32)
    state = dict(
        it=jnp.int32(0),
        lo=jnp.where(pos_side, 0.0, smin), flo=jnp.where(pos_side, fge0, n_adm),
        hi=jnp.where(neg_side, 0.0, from_key(to_key(smax) + 1)), fhi=jnp.where(neg_side, fge0, 0),
        wlo=jnp.ones((1, tq), F32), whi=jnp.ones((1, tq), F32), last=jnp.zeros((1, tq), I32),
        done=as_i(jnp.logical_or(take_all, jnp.logical_or(zero_cut, zero_tie))),
        thr=jnp.where(take_all, neg_inf, 0.0),
        tie=as_i(jnp.logical_and(zero_tie, jnp.logical_not(take_all))),
        need=n_sel - fgt0)

    def searching(st):
        return jnp.logical_and(jnp.min(st["done"]) == 0, st["it"] < SEARCH_CAP)

    def search_step(st):
        lo, hi, flo, fhi = st["lo"], st["hi"], st["flo"], st["fhi"]
        flo_f, fhi_f = flo.astype(F32), fhi.astype(F32)
        narrow = flo - fhi <= NARROW_BRACKET
        glo = jnp.where(narrow, flo_f - k_f, jnp.log(flo_f + 0.5) - jnp.log(k_f)) * st["wlo"]
        ghi = jnp.where(narrow, k_f - fhi_f, jnp.log(k_f) - jnp.log(fhi_f + 0.5)) * st["whi"]
        frac = jnp.clip(glo / (glo + ghi), 0.005, 0.995)
        cand = lo + (hi - lo) * frac
        klo, khi, kc = to_key(lo), to_key(hi), to_key(cand)
        mid = (lax.shift_right_arithmetic(klo, 1) + lax.shift_right_arithmetic(khi, 1) + (klo & khi & 1))
        outside = jnp.logical_or(jnp.logical_or(kc <= klo, kc >= khi), (st["it"] & 3) == 3)
        cand = jnp.where(outside, from_key(mid), cand)
        adjacent = klo + 1 >= khi
        (cnt,) = count(lambda s, r0: s >= cand)
        open_ = st["done"] == 0
        active = jnp.logical_and(open_, jnp.logical_not(adjacent))
        up = jnp.logical_and(active, cnt > n_sel)
        dn = jnp.logical_and(active, cnt < n_sel)
        hit = jnp.logical_and(active, cnt == n_sel)
        tied = jnp.logical_and(open_, adjacent)
        last = st["last"]
        return dict(
            it=st["it"] + 1,
            lo=jnp.where(up, cand, lo), flo=jnp.where(up, cnt, flo),
            hi=jnp.where(dn, cand, hi), fhi=jnp.where(dn, cnt, fhi),
            wlo=jnp.where(dn, jnp.where(last == -1, st["wlo"] * 0.5, 1.0), jnp.where(up, 1.0, st["wlo"])),
            whi=jnp.where(up, jnp.where(last == 1, st["whi"] * 0.5, 1.0), jnp.where(dn, 1.0, st["whi"])),
            last=jnp.where(up, 1, jnp.where(dn, -1, last)),
            done=jnp.where(jnp.logical_or(hit, tied), 1, st["done"]),
            thr=jnp.where(hit, cand, jnp.where(tied, lo, st["thr"])),
            tie=jnp.where(tied, 1, st["tie"]),
            need=jnp.where(tied, n_sel - fhi, st["need"]))

    state = lax.while_loop(searching, search_step, state)
    thr, tie, need = state["thr"], state["tie"], state["need"]

    def tie_cut():
        def step(t, c0):
            cand = c0 | lax.shift_left(jnp.int32(1), idx_bits - 1 - t)
            (cnt,) = count(lambda s, r0: jnp.where(s == thr, row_iota2 + r0, cand) < cand)
            return jnp.where(cnt < need, cand, c0)
        return lax.fori_loop(0, idx_bits, step, jnp.zeros((1, tq), I32))

    any_tie = jnp.max(tie) > 0
    cut = lax.cond(any_tie, tie_cut, lambda: jnp.zeros((1, tq), I32))
    cut = jnp.where(jnp.logical_or(take_all, zero_cut), -1, jnp.where(tie > 0, cut, jnp.int32(2 ** 30)))

    acc_ref[...] = jnp.zeros(acc_ref.shape, F32)

    def masked_scores(j):
        r0 = pl.multiple_of(j * tk, tk)
        sc = sc_ref[pl.ds(r0, tk), :]
        bias = jnp.where(sc > thr, 0.0,
                         jnp.where(sc == thr, jnp.where(row_iota + r0 <= cut, 0.0, MASKED), MASKED))
        bias = jnp.concatenate([bias] * GROUP_A, axis=1)
        return [_dot(ka_ref[g, pl.ds(r0, tk), :], qa_ref[g]) + bias for g in range(N_KV_A)]

    def attend_bounded():
        def tiles(js, _):
            pv = [0.0] * N_KV_A
            for c in range(SUPER):
                j = js * SUPER + c
                for g, s in enumerate(masked_scores(j)):
                    pv[g] = pv[g] + _dot(va_ref[g, j], jnp.exp2(s).astype(BF16))
            for g in range(N_KV_A):
                acc_ref[g] += pv[g]
            return 0
        lax.fori_loop(0, n_super, tiles, 0)

    def attend_online():
        def tile(j, ms):
            new_ms = []
            for g, s in enumerate(masked_scores(j)):
                m_new = jnp.maximum(ms[g], jnp.max(s, axis=0, keepdims=True))
                p = jnp.exp2(s - m_new).astype(BF16)
                acc_ref[g] = acc_ref[g] * jnp.exp2(ms[g] - m_new) + _dot(va_ref[g, j], p)
                new_ms.append(m_new)
            return tuple(new_ms)
        m0 = jnp.full((1, GROUP_A * tq), MASKED, F32)
        lax.fori_loop(0, n_tiles, tile, (m0,) * N_KV_A)

    @pl.when(i == 0)
    def _():
        for g in range(N_KV_A):
            kmax_ref[g] = _max_row_norm(ka_ref.at[g], SUPER * tk, kmax_ref.shape[1:])

    score_bound = jnp.float32(0.0)
    for g in range(N_KV_A):
        qf = qa_ref[g].astype(F32)
        qn = jnp.sqrt(jnp.sum(qf * qf, axis=0, keepdims=True))
        score_bound = jnp.maximum(score_bound, jnp.max(qn * kmax_ref[g, 0:1, 0:1]))
    lax.cond(score_bound <= SAFE_EXP2, attend_bounded, attend_online)
    for g in range(N_KV_A):
        o_ref[g] = acc_ref[g, :HEAD_DIM] / acc_ref[g, HEAD_DIM:HEAD_DIM + 1]


def _dsa(qi_t, w_t, qa_t, ki, ka, va_t, *, pos0, n_keys, n_sel, tq, tk):
    b, nq = qi_t.shape[:2]
    lp = ki.shape[1]
    nt = lp // tk
    sq = None
    kern = functools.partial(_dsa_kernel, pos0=pos0, n_keys=n_keys, n_sel=n_sel, tq=tq, tk=tk,
                             idx_bits=int(np.ceil(np.log2(lp))))
    per_batch = lambda shape: pl.BlockSpec((sq,) + shape, lambda bi, i: (bi,) + (0,) * len(shape),
                                           pipeline_mode=pl.Buffered(1))
    per_q = lambda shape: pl.BlockSpec((sq, sq) + shape, lambda bi, i: (bi, i) + (0,) * len(shape))
    return pl.pallas_call(
        kern,
        grid=(b, nq),
        in_specs=[per_q((IDX_DIM, N_IDX_HEADS * tq)), per_q((N_IDX_HEADS, tq)),
                  per_q((N_KV_A, HEAD_DIM, GROUP_A * tq)),
                  per_batch((lp, IDX_DIM)), per_batch((N_KV_A, lp, HEAD_DIM)),
                  per_batch((N_KV_A, nt, V_ROWS, tk))],
        out_specs=per_q((N_KV_A, HEAD_DIM, GROUP_A * tq)),
        out_shape=jax.ShapeDtypeStruct((b, nq, N_KV_A, HEAD_DIM, GROUP_A * tq), F32),
        scratch_shapes=[pltpu.VMEM((lp, tq), F32),
                        pltpu.VMEM((N_KV_A, V_ROWS, GROUP_A * tq), F32),
                        pltpu.VMEM((N_KV_A, SUBLANES, LANES), F32)],
        compiler_params=pltpu.CompilerParams(dimension_semantics=("arbitrary", "arbitrary"),
                                             vmem_limit_bytes=VMEM_LIMIT_BYTES),
        name="dsa",
    )(qi_t, w_t, qa_t, ki, ka, va_t)


def _stick_kernel(q_ref, k_ref, v_ref, tri_ref, o_ref, kmax_ref, *, pos0, tq, tk, heads):
    i = pl.program_id(2)

    @pl.when(i == 0)
    def _():
        for h in range(heads):
            kmax_ref[h] = _max_row_norm(k_ref.at[h], SUPER * tk, kmax_ref.shape[1:])

    bounds = []
    for h in range(heads):
        qf = q_ref[h].astype(F32)
        qn = jnp.sqrt(jnp.sum(qf * qf, axis=0, keepdims=True))
        bounds.append(qn * kmax_ref[h, 0:1, 0:1] * STICK_BOUND_SLACK)
    pos_first = pos0 + i * tq
    pos_row = pos_first + lax.broadcasted_iota(I32, (1, tq), 1)
    row_iota = lax.broadcasted_iota(I32, (tk, tq), 0)
    j_first = jnp.maximum(pos_first + tq - 2, 0) // tk
    tri = tri_ref[...]
    o_ref[...] = jnp.zeros(o_ref.shape, F32)

    def cond(state):
        j, live, _ = state
        return jnp.logical_and(j >= 0, live > 0)

    def body(state):
        j, _, carries = state
        r0 = pl.multiple_of(j * tk, tk)
        vis = row_iota + r0 < pos_row
        hs = range(heads)
        z = [_dot(k_ref[h, pl.ds(r0, tk), :], q_ref[h]) for h in hs]
        softplus = [jnp.maximum(z[h], 0.0) + jnp.log2(1.0 + jnp.exp2(-jnp.abs(z[h]))) for h in hs]
        log_keep = [jnp.where(vis, -softplus[h], 0.0) for h in hs]
        incl = [_dot_hilo_left(tri, log_keep[h]) for h in hs]
        a = [jnp.where(vis, jnp.exp2(z[h] + incl[h] + carries[h]), 0.0).astype(BF16) for h in hs]
        pv = [_dot(v_ref[h, j], a[h]) for h in hs]
        for h in hs:
            o_ref[h] += pv[h]
        new_carries = [carries[h] + incl[h][0:1, :] for h in hs]
        worst = bounds[0] + new_carries[0]
        for h in range(1, heads):
            worst = jnp.maximum(worst, bounds[h] + new_carries[h])
        live = (jnp.max(worst) >= STICK_CUTOFF).astype(I32)
        return j - 1, live, tuple(new_carries)

    init = (j_first, jnp.int32(1), (jnp.zeros((1, tq), F32),) * heads)
    lax.while_loop(cond, body, init)


def _dot_hilo_left(m, x):
    hi = x.astype(BF16)
    lo = (x - hi.astype(F32)).astype(BF16)
    return _dot(m, hi) + _dot(m, lo)


def _stick(q_t, k, v_t, *, pos0, tq, tk):
    b, nh, _, t = q_t.shape
    lp = k.shape[2]
    nt = lp // tk
    sq = None
    tri = jnp.asarray(np.triu(np.ones((tk, tk), np.float32)), BF16)
    hb = STICK_HEADS
    assert nh % hb == 0
    kern = functools.partial(_stick_kernel, pos0=pos0, tq=tq, tk=tk, heads=hb)
    once = pl.Buffered(1)
    return pl.pallas_call(
        kern,
        grid=(b, nh // hb, t // tq),
        in_specs=[pl.BlockSpec((sq, hb, HEAD_DIM, tq), lambda bi, h, i: (bi, h, 0, i)),
                  pl.BlockSpec((sq, hb, lp, HEAD_DIM), lambda bi, h, i: (bi, h, 0, 0), pipeline_mode=once),
                  pl.BlockSpec((sq, hb, nt, HEAD_DIM, tk), lambda bi, h, i: (bi, h, 0, 0, 0),
                               pipeline_mode=once),
                  _const_spec((tk, tk))],
        out_specs=pl.BlockSpec((sq, hb, HEAD_DIM, tq), lambda bi, h, i: (bi, h, 0, i)),
        out_shape=jax.ShapeDtypeStruct((b, nh, HEAD_DIM, t), F32),
        scratch_shapes=[pltpu.VMEM((hb, SUBLANES, LANES), F32)],
        compiler_params=pltpu.CompilerParams(dimension_semantics=("arbitrary",) * 3,
                                             vmem_limit_bytes=VMEM_LIMIT_BYTES),
        name="stick",
    )(q_t, k, v_t, tri)


def _rms(x, g):
    ms = jnp.mean(x * x, axis=-1, keepdims=True)
    return (x * lax.rsqrt(ms + EPS)) * g


def _tail_kernel(x_ref, oa_ref, ob_ref, ga_ref, gb_ref, p_ref, pa_ref, pb_ref, wo_ref,
                 nf_ref, wu_ref, wd_ref, np_ref, wp_ref, wg_ref, y_ref, *, ff_chunk):
    mix = (jax.nn.sigmoid(ga_ref[...]) * _dot(oa_ref[...], pa_ref[...])
           + jax.nn.sigmoid(gb_ref[...]) * _dot(ob_ref[...], pb_ref[...]))
    x = x_ref[...] + _dot(mix.astype(BF16), wo_ref[...])
    h = _rms(x, nf_ref[...]).astype(BF16)
    d_ff = wu_ref.shape[1]
    ffn = jnp.zeros_like(x)
    for c in range(d_ff // ff_chunk):
        u = jnp.maximum(_dot(h, wu_ref[:, c * ff_chunk:(c + 1) * ff_chunk]), 0.0)
        ffn = ffn + _dot((u * u).astype(BF16), wd_ref[c * ff_chunk:(c + 1) * ff_chunk, :])
    x = x + ffn
    h = _rms(x, np_ref[...]).astype(BF16)
    y_ref[...] = x + _dot(p_ref[...].astype(BF16), wp_ref[...]) * jax.nn.sigmoid(_dot(h, wg_ref[...]))


def _tail(x2d, o_a, o_b, ga, gb, p2d, proj_a, proj_b, w_out, norm_ffn, w_up, w_down,
          norm_ple, w_ple, w_gate):
    n, d_model = x2d.shape
    tm = min(TOKEN_TILE, n)
    assert n % tm == 0
    d_ff = w_up.shape[1]
    row = lambda w: pl.BlockSpec((tm, w), lambda i: (i, 0))
    weights = (proj_a, proj_b, w_out, norm_ffn.reshape(1, d_model), w_up, w_down,
               norm_ple.reshape(1, d_model), w_ple, w_gate)
    return pl.pallas_call(
        functools.partial(_tail_kernel, ff_chunk=min(1024, d_ff)),
        grid=(n // tm,),
        in_specs=[row(d_model), row(WIDTH_A), row(WIDTH_B), row(d_model), row(d_model),
                  row(p2d.shape[1])] + [_const_spec(w.shape) for w in weights],
        out_specs=row(d_model),
        out_shape=jax.ShapeDtypeStruct((n, d_model), F32),
        compiler_params=pltpu.CompilerParams(dimension_semantics=("arbitrary",),
                                             vmem_limit_bytes=VMEM_LIMIT_BYTES),
        name="tail",
    )(x2d, o_a, o_b, ga, gb, p2d, *weights)


def _pad_keys(a, lp):
    return jnp.pad(a, ((0, 0), (0, lp - a.shape[1])) + ((0, 0),) * (a.ndim - 2))


def _layer(x, p, pos0, caches, wts):
    (norm_mix, w_in_p, g_qa, g_ka, proj_a, proj_b, w_out, norm_ffn, w_up, w_down,
     norm_ple, w_ple, w_gate) = wts
    b, t, d_model = x.shape
    n = b * t
    pos = pos0 + jnp.arange(t, dtype=I32)
    x2d = x.reshape(n, d_model)
    qa, ka, va, qi, kiwi, qb, kb, vb, ga, gb = _project(x2d, t, pos, norm_mix, w_in_p, g_qa, g_ka)

    ka = ka.reshape(b, t, N_KV_A, HEAD_DIM)
    va = va.reshape(b, t, N_KV_A, HEAD_DIM)
    ki = kiwi[:, :IDX_DIM].reshape(b, t, IDX_DIM)
    kb = kb.reshape(b, t, N_HEADS_B, HEAD_DIM)
    vb = vb.reshape(b, t, N_HEADS_B, HEAD_DIM)
    new_state = (ka, va, ki, kb, vb)
    wi = kiwi[:, IDX_DIM:IDX_DIM + N_IDX_HEADS] * ((N_IDX_HEADS ** -0.5) * (IDX_DIM ** -0.5))

    if caches is not None:
        ka_all, va_all, ki_all, kb_all, vb_all = (
            jnp.concatenate([c.astype(BF16), s.astype(BF16)], axis=1) for c, s in zip(caches, new_state))
    else:
        ka_all, va_all, ki_all, kb_all, vb_all = (s.astype(BF16) for s in new_state)
    n_keys = ka_all.shape[1]
    tk = KEY_TILE
    lp = -(-n_keys // (SUPER * tk)) * (SUPER * tk)
    nt = lp // tk
    n_sel = min(TOPK_MAX, n_keys // 4)

    tq = min(Q_TILE_A, t)
    nq = t // tq
    qi_t = qi.reshape(b, nq, tq, N_IDX_HEADS, IDX_DIM).transpose(0, 1, 4, 3, 2).reshape(
        b, nq, IDX_DIM, N_IDX_HEADS * tq)
    w_t = wi.reshape(b, nq, tq, N_IDX_HEADS).transpose(0, 1, 3, 2)
    qa_t = qa.reshape(b, nq, tq, N_KV_A, GROUP_A, HEAD_DIM).transpose(0, 1, 3, 5, 4, 2).reshape(
        b, nq, N_KV_A, HEAD_DIM, GROUP_A * tq)
    ki_p = _pad_keys(ki_all, lp)
    ka_p = _pad_keys(ka_all, lp).transpose(0, 2, 1, 3)
    va_t = _pad_keys(va_all, lp).reshape(b, nt, tk, N_KV_A, HEAD_DIM).transpose(0, 3, 1, 4, 2)
    va_t = jnp.concatenate([va_t, jnp.ones((b, N_KV_A, nt, V_ROWS - HEAD_DIM, tk), BF16)], axis=3)
    oa_t = _dsa(qi_t, w_t, qa_t, ki_p, ka_p, va_t, pos0=pos0, n_keys=n_keys, n_sel=n_sel, tq=tq, tk=tk)
    o_a = oa_t.reshape(b, nq, N_KV_A, HEAD_DIM, GROUP_A, tq).transpose(0, 1, 5, 2, 4, 3).reshape(
        n, WIDTH_A).astype(BF16)

    tqb = min(Q_TILE_B, t)
    qb_t = qb.reshape(b, t, N_HEADS_B, HEAD_DIM).transpose(0, 2, 3, 1)
    kb_p = _pad_keys(kb_all, lp).transpose(0, 2, 1, 3)
    vb_t = _pad_keys(vb_all, lp).reshape(b, nt, tk, N_HEADS_B, HEAD_DIM).transpose(0, 3, 1, 4, 2)
    ob_t = _stick(qb_t, kb_p, vb_t, pos0=pos0, tq=tqb, tk=tk)
    o_b = ob_t.transpose(0, 3, 1, 2).reshape(n, WIDTH_B).astype(BF16)

    y = _tail(x2d, o_a, o_b, ga, gb, p.reshape(n, p.shape[-1]), proj_a, proj_b, w_out, norm_ffn,
              w_up, w_down, norm_ple, w_ple, w_gate)
    return y.reshape(b, t, d_model), new_state


def kernel(x_prompt, x_sample, cache_a_k, cache_a_v, cache_a_kidx, cache_b_k, cache_b_v, p_prompt, p_sample, norm_mix, w_in, g_qa, g_ka, proj_a, proj_b, w_out, norm_ffn, w_up, w_down, norm_ple, w_ple, w_ple_gate):
    depth = w_in.shape[0]
    past = cache_a_k.shape[2]
    d_model = x_prompt.shape[-1]
    split = _SEG["kiwi"][0] + IDX_DIM + N_IDX_HEADS
    assert w_in.shape[2] == split + 3 * WIDTH_B + 2 * d_model
    xp, xs = x_prompt, x_sample
    outs_p, outs_s = [], []
    for i in range(depth):
        w_in_p = jnp.concatenate(
            [w_in[i, :, :split], jnp.zeros((d_model, LANES - IDX_DIM - N_IDX_HEADS), w_in.dtype),
             w_in[i, :, split:]], axis=1).astype(BF16)
        wts = (norm_mix[i], w_in_p, g_qa[i], g_ka[i], proj_a[i].astype(BF16), proj_b[i].astype(BF16),
               w_out[i].astype(BF16), norm_ffn[i], w_up[i].astype(BF16), w_down[i].astype(BF16),
               norm_ple[i], w_ple[i].astype(BF16), w_ple_gate[i].astype(BF16))
        xp, st = _layer(xp, p_prompt[i], 0, None, wts)
        outs_p.append(st)
        caches = (cache_a_k[i], cache_a_v[i], cache_a_kidx[i], cache_b_k[i], cache_b_v[i])
        xs, st = _layer(xs, p_sample[i], past, caches, wts)
        outs_s.append(st)
    stack = lambda outs, k: jnp.stack([o[k] for o in outs], 0)
    return ((xp, xs) + tuple(stack(outs_p, k) for k in range(5))
            + tuple(stack(outs_s, k) for k in range(5)))
```

```python
import functools

import numpy as np
import jax
import jax.numpy as jnp
from jax import lax
from jax.experimental import pallas as pl
from jax.experimental.pallas import tpu as pltpu

CHUNK = 64
HEAD_DIM = 64
N_HEADS_A = 8
N_KV_A = 2
GROUP_A = N_HEADS_A // N_KV_A
N_HEADS_B = 8
N_IDX_HEADS = 8
IDX_DIM = 32
TOPK_MAX = 256
ROPE_THETA = 500000.0
EPS = 1e-6
WIDTH_A = N_HEADS_A * HEAD_DIM
WIDTH_B = N_HEADS_B * HEAD_DIM
WIDTH_I = N_IDX_HEADS * IDX_DIM
KV_A = N_KV_A * HEAD_DIM

LANES = 128
SUBLANES = 8
BF16_ROWS = 16
VMEM_LIMIT_BYTES = 56 * 1024 * 1024

TOKEN_TILE = 256
KEY_TILE = 256
Q_TILE_A = 128
Q_TILE_B = 256
SUPER = 4
COUNT_ROWS = 64
STICK_HEADS = 4
LOG2E = 1.4426950408889634
SAFE_EXP2 = 60.0
NORM_SLACK = 1.01
SEARCH_CAP = 200
NARROW_BRACKET = 48
SUB_ULP_STEPS = 6

MASKED = -1e30
STICK_CUTOFF = -120.0 * LOG2E
STICK_BOUND_SLACK = 1.01

F32 = jnp.float32
BF16 = jnp.bfloat16
I32 = jnp.int32


def _const_spec(shape):
    nd = len(shape)
    return pl.BlockSpec(shape, lambda *_: (0,) * nd, pipeline_mode=pl.Buffered(1))


def _dot(a, b):
    return jnp.dot(a, b, preferred_element_type=F32)


def _dot_hilo(x, m):
    hi = x.astype(BF16)
    lo = (x - hi.astype(F32)).astype(BF16)
    return _dot(hi, m) + _dot(lo, m)


def _dot_hilo_left(m, x):
    hi = x.astype(BF16)
    lo = (x - hi.astype(F32)).astype(BF16)
    return _dot(m, hi) + _dot(m, lo)


def _block_ones(n_blocks, value=1.0):
    return jnp.asarray(np.kron(np.eye(n_blocks, dtype=np.float32),
                               np.full((HEAD_DIM, HEAD_DIM), value, np.float32)), BF16)


def _max_head_norms(k_ref, ones_ref, chunk):
    ones = ones_ref[...]

    def body(j, mx):
        kf = k_ref[pl.ds(pl.multiple_of(j * chunk, chunk), chunk), :].astype(F32)
        sq = _dot((kf * kf).astype(BF16), ones)
        part = sq[0:SUBLANES]
        for r in range(1, chunk // SUBLANES):
            part = jnp.maximum(part, sq[r * SUBLANES:(r + 1) * SUBLANES])
        return jnp.maximum(mx, part)
    mx = lax.fori_loop(0, k_ref.shape[0] // chunk, body, jnp.zeros((SUBLANES, k_ref.shape[1]), F32))
    return jnp.sqrt(jnp.max(mx, axis=0, keepdims=True) * NORM_SLACK)


_SEG = {}
_off = 0
for _name, _w in (("qa", WIDTH_A), ("ka", KV_A), ("va", KV_A), ("qi", WIDTH_I),
                  ("kiwi", LANES), ("qb", WIDTH_B), ("kb", WIDTH_B), ("vb", WIDTH_B), ("gates", 0)):
    _SEG[_name] = (_off, _w)
    _off += _w
del _off, _name, _w
IDX_WEIGHT_SCALE = (N_IDX_HEADS ** -0.5) * (IDX_DIM ** -0.5)
Q_SCALE = HEAD_DIM ** -0.5 * LOG2E


def _rope(y, c, s1, s2, half):
    cols = []
    for k in range(y.shape[1] // LANES):
        yk = y[:, k * LANES:(k + 1) * LANES]
        cols.append(yk * c + pltpu.roll(yk, LANES - half, 1) * s1 + pltpu.roll(yk, half, 1) * s2)
    return cols[0] if len(cols) == 1 else jnp.concatenate(cols, axis=1)


def _proj_kernel(x_ref, g_ref, w_ref, gq_ref, gk_ref, gm_ref, tab_ref,
                 ka_ref, va_ref, kiwi_ref, kb_ref, vb_ref, ga_ref, gb_ref,
                 kab_ref, kiwib_ref, kbb_ref, qa_ref, qi_ref, qb_ref, w_out_ref, vab_ref, vbb_ref,
                 *, d_model, feature_major):
    x = x_ref[...]
    ms = jnp.mean(x * x, axis=-1, keepdims=True)
    h = ((x * lax.rsqrt(ms + EPS)) * g_ref[...]).astype(BF16)

    def seg(name):
        off, w = _SEG[name]
        return _dot(h, w_ref[:, off:off + w])

    def tab(k):
        return tab_ref[:, k * LANES:(k + 1) * LANES]

    def head_norm(u, gain):
        w = u.shape[1]
        msq = _dot_hilo(u * u, gm_ref[:w, :w])
        return (u * lax.rsqrt(msq + EPS)) * gain

    def put(ref, val):
        ref[...] = (val.T if feature_major else val).astype(ref.dtype)

    qa = head_norm(seg("qa"), gq_ref[...])
    put(qa_ref, _rope(qa, tab(0), tab(1), tab(2), HEAD_DIM // 8) * Q_SCALE)
    ka = _rope(head_norm(seg("ka"), gk_ref[...]), tab(0), tab(1), tab(2), HEAD_DIM // 8)
    ka_ref[...] = ka
    kab_ref[...] = ka.astype(BF16)
    va = seg("va")
    va_ref[...] = va
    put(vab_ref, va)
    put(qi_ref, _rope(seg("qi"), tab(3), tab(4), tab(5), IDX_DIM // 8))
    kiwi = _rope(seg("kiwi"), tab(6), tab(7), tab(8), IDX_DIM // 8)
    kiwi_ref[...] = kiwi
    kiwib_ref[...] = kiwi.astype(BF16)
    wi = kiwi * IDX_WEIGHT_SCALE
    if feature_major:
        w_out_ref[...] = wi.T[IDX_DIM:IDX_DIM + N_IDX_HEADS, :]
    else:
        w_out_ref[...] = wi
    put(qb_ref, seg("qb") * Q_SCALE)
    kb = seg("kb")
    kb_ref[...] = kb
    kbb_ref[...] = kb.astype(BF16)
    vb = seg("vb")
    vb_ref[...] = vb
    put(vbb_ref, vb)
    g_off = _SEG["gates"][0]
    ga_ref[...] = _dot(h, w_ref[:, g_off:g_off + d_model])
    gb_ref[...] = _dot(h, w_ref[:, g_off + d_model:g_off + 2 * d_model])


def _rope_tables(pos):
    def one(d):
        r = d // 4
        hr = r // 2
        inv = jnp.power(jnp.float32(ROPE_THETA), -jnp.arange(hr, dtype=F32) * (2.0 / r))
        ang = pos.astype(F32)[:, None] * inv[None, :]
        cos, sin = jnp.cos(ang), jnp.sin(ang)
        t = pos.shape[0]
        one_ = jnp.ones((t, d - r), F32)
        zero = jnp.zeros((t, d - r), F32)
        zh = jnp.zeros((t, hr), F32)
        c = jnp.concatenate([cos, cos, one_], axis=1)
        s1 = jnp.concatenate([-sin, zh, zero], axis=1)
        s2 = jnp.concatenate([zh, sin, zero], axis=1)
        return c, s1, s2

    c64, s164, s264 = one(HEAD_DIM)
    c32, s132, s232 = one(IDX_DIM)
    t = pos.shape[0]
    rep = lambda a, d: jnp.tile(a, (1, LANES // d))
    padc = jnp.ones((t, LANES - IDX_DIM), F32)
    padz = jnp.zeros((t, LANES - IDX_DIM), F32)
    return jnp.concatenate([
        rep(c64, HEAD_DIM), rep(s164, HEAD_DIM), rep(s264, HEAD_DIM),
        rep(c32, IDX_DIM), rep(s132, IDX_DIM), rep(s232, IDX_DIM),
        jnp.concatenate([c32, padc], axis=1), jnp.concatenate([s132, padz], axis=1),
        jnp.concatenate([s232, padz], axis=1)], axis=1)


def _project(x, pos, norm_mix, w_in_p, g_qa, g_ka):
    b, t, d_model = x.shape
    n = b * t
    tm = min(TOKEN_TILE, n)
    assert n % tm == 0 and (t % tm == 0 or tm % t == 0)
    feature_major = t % tm == 0
    tables = _rope_tables(pos)
    if t < tm:
        tables = jnp.tile(tables, (tm // t, 1))
    tiles_per_seq = max(t // tm, 1)
    gm = _block_ones(N_HEADS_A, 1.0 / HEAD_DIM)
    gq = jnp.tile(g_qa.reshape(1, HEAD_DIM), (1, N_HEADS_A))
    gk = jnp.tile(g_ka.reshape(1, HEAD_DIM), (1, N_KV_A))

    row = lambda w: pl.BlockSpec((tm, w), lambda i: (i, 0))
    rows = lambda w, dt: (jax.ShapeDtypeStruct((n, w), dt), row(w))
    if feature_major:
        cols = lambda w, dt: (jax.ShapeDtypeStruct((b, w, t), dt),
                              pl.BlockSpec((None, w, tm), lambda i: (i // tiles_per_seq, 0, i % tiles_per_seq)))
        w_out = cols(N_IDX_HEADS, F32)
    else:
        cols = rows
        w_out = rows(LANES, F32)
    names = ["ka", "va", "kiwi", "kb", "vb", "ga", "gb", "ka_bf", "kiwi_bf", "kb_bf",
             "qa", "qi", "qb", "w", "va_bf", "vb_bf"]
    outs = [rows(KV_A, F32), rows(KV_A, F32), rows(LANES, F32), rows(WIDTH_B, F32), rows(WIDTH_B, F32),
            rows(d_model, F32), rows(d_model, F32), rows(KV_A, BF16), rows(LANES, BF16), rows(WIDTH_B, BF16),
            cols(WIDTH_A, BF16), cols(WIDTH_I, BF16), cols(WIDTH_B, BF16), w_out,
            cols(KV_A, BF16), cols(WIDTH_B, BF16)]
    res = pl.pallas_call(
        functools.partial(_proj_kernel, d_model=d_model, feature_major=feature_major),
        grid=(n // tm,),
        in_specs=[row(d_model), _const_spec((1, d_model)), _const_spec(w_in_p.shape),
                  _const_spec((1, WIDTH_A)), _const_spec((1, KV_A)), _const_spec((WIDTH_A, WIDTH_A)),
                  pl.BlockSpec((tm, 9 * LANES), lambda i: (i % tiles_per_seq, 0))],
        out_specs=[o[1] for o in outs],
        out_shape=[o[0] for o in outs],
        compiler_params=pltpu.CompilerParams(dimension_semantics=("arbitrary",),
                                             vmem_limit_bytes=VMEM_LIMIT_BYTES),
        name="proj",
    )(x.reshape(n, d_model), norm_mix.reshape(1, d_model), w_in_p, gq, gk, gm, tables)
    p = dict(zip(names, res))
    if not feature_major:
        to_cols = lambda a: a.reshape(b, t, a.shape[-1]).transpose(0, 2, 1)
        for k in ("qa", "qi", "qb", "va_bf", "vb_bf"):
            p[k] = to_cols(p[k])
        p["w"] = to_cols(p["w"][:, IDX_DIM:IDX_DIM + N_IDX_HEADS])
    return p


def _dsa_kernel(qi_ref, w_ref, qa_ref, ki_ref, ka_ref, va_ref, ones_ref, o_ref,
                sc_ref, acc_ref, kmax_ref, *, pos0, n_keys, n_sel, tq, tk, idx_bits):
    i = pl.program_id(1)
    pos_first = pos0 + i * tq
    lim_max = jnp.minimum(((pos_first + tq - 1) // CHUNK + 1) * CHUNK, n_keys)
    n_tiles = (lim_max + tk - 1) // tk
    lane_pos = pos_first + lax.broadcasted_iota(I32, (1, tq), 1)
    lim_row = jnp.minimum((lax.shift_right_arithmetic(lane_pos, int(np.log2(CHUNK))) + 1) * CHUNK, n_keys)
    neg_inf = jnp.float32(-jnp.inf)

    n_super = (n_tiles + SUPER - 1) // SUPER
    row_iota = lax.broadcasted_iota(I32, (tk, tq), 0)

    def lanes_by_head(ref, first, n_heads, width):
        return jnp.concatenate([ref[(first + h) * width:(first + h + 1) * width, :] for h in range(n_heads)],
                               axis=1)

    qi = lanes_by_head(qi_ref, 0, N_IDX_HEADS, IDX_DIM)
    qi = jnp.concatenate([qi, jnp.zeros((LANES - IDX_DIM, N_IDX_HEADS * tq), BF16)], axis=0)
    blank = jnp.zeros((HEAD_DIM, GROUP_A * tq), BF16)
    qa = []
    for g in range(N_KV_A):
        parts = [blank] * N_KV_A
        parts[g] = lanes_by_head(qa_ref, g * GROUP_A, GROUP_A, HEAD_DIM)
        qa.append(jnp.concatenate(parts, axis=0))
    w = w_ref[...]

    def fold_rows(x, op):
        out = x[0:SUBLANES]
        for r in range(1, x.shape[0] // SUBLANES):
            out = op(out, x[r * SUBLANES:(r + 1) * SUBLANES])
        return out

    def score_tiles(js, carry):
        smax, smin = carry
        for c in range(SUPER):
            r0 = pl.multiple_of((js * SUPER + c) * tk, tk)
            lg = _dot(ki_ref[pl.ds(r0, tk), :], qi)
            sc = jnp.zeros((tk, tq), F32)
            for h in range(N_IDX_HEADS):
                sc = sc + jnp.maximum(lg[:, h * tq:(h + 1) * tq], 0.0) * w[h:h + 1, :]
            adm = row_iota + r0 < lim_row
            masked = jnp.where(adm, sc, neg_inf)
            sc_ref[pl.ds(r0, tk), :] = masked
            smax = jnp.maximum(smax, fold_rows(masked, jnp.maximum))
            smin = jnp.minimum(smin, fold_rows(jnp.where(adm, sc, -neg_inf), jnp.minimum))
        return smax, smin

    smax, smin = lax.fori_loop(0, n_super, score_tiles,
                               (jnp.full((SUBLANES, tq), neg_inf, F32), jnp.full((SUBLANES, tq), -neg_inf, F32)))
    smax = jnp.max(smax, axis=0, keepdims=True)
    smin = jnp.min(smin, axis=0, keepdims=True)

    n_steps = (n_tiles + 1) // 2
    tk2 = 2 * tk
    row_iota2 = lax.broadcasted_iota(I32, (tk2, tq), 0)

    def count(*preds):
        def body(j, accs):
            r0 = pl.multiple_of(j * tk2, tk2)
            s = sc_ref[pl.ds(r0, tk2), :]
            out = []
            for pred, acc in zip(preds, accs):
                hit = jnp.where(pred(s, r0), 1, 0).astype(I32)
                for r in range(tk2 // COUNT_ROWS):
                    acc = acc + hit[r * COUNT_ROWS:(r + 1) * COUNT_ROWS]
                out.append(acc)
            return tuple(out)
        accs = lax.fori_loop(0, n_steps, body, (jnp.zeros((COUNT_ROWS, tq), I32),) * len(preds))
        return [jnp.sum(a, axis=0, keepdims=True) for a in accs]

    def to_key(x):
        bits = pltpu.bitcast(x, I32)
        return bits ^ (lax.shift_right_arithmetic(bits, 31) & 0x7FFFFFFF)

    def from_key(k):
        return pltpu.bitcast(k ^ (lax.shift_right_arithmetic(k, 31) & 0x7FFFFFFF), F32)

    k_f = jnp.float32(n_sel)
    n_adm = lim_row
    zero = jnp.zeros((1, tq), F32)
    fge0, fgt0 = count(lambda s, r0: s >= zero, lambda s, r0: s > zero)
    take_all = n_adm <= n_sel
    zero_cut = fgt0 == n_sel
    zero_tie = jnp.logical_and(fgt0 < n_sel, fge0 >= n_sel)
    pos_side = fgt0 > n_sel
    neg_side = fge0 < n_sel
    as_i = lambda m: jnp.where(m, 1, 0).astype(I32)
    state = dict(
        it=jnp.int32(0),
        lo=jnp.where(pos_side, 0.0, smin), flo=jnp.where(pos_side, fge0, n_adm),
        hi=jnp.where(neg_side, 0.0, from_key(to_key(smax) + 1)), fhi=jnp.where(neg_side, fge0, 0),
        wlo=jnp.ones((1, tq), F32), whi=jnp.ones((1, tq), F32), last=jnp.zeros((1, tq), I32),
        stuck=jnp.zeros((1, tq), I32),
        done=as_i(jnp.logical_or(take_all, jnp.logical_or(zero_cut, zero_tie))),
        thr=jnp.where(take_all, neg_inf, 0.0),
        tie=as_i(jnp.logical_and(zero_tie, jnp.logical_not(take_all))),
        need=n_sel - fgt0)

    def searching(st):
        return jnp.logical_and(jnp.min(st["done"]) == 0, st["it"] < SEARCH_CAP)

    def search_step(st):
        lo, hi, flo, fhi = st["lo"], st["hi"], st["flo"], st["fhi"]
        flo_f, fhi_f = flo.astype(F32), fhi.astype(F32)
        narrow = flo - fhi <= NARROW_BRACKET
        glo = jnp.where(narrow, flo_f - k_f, jnp.log(flo_f + 0.5) - jnp.log(k_f)) * st["wlo"]
        ghi = jnp.where(narrow, k_f - fhi_f, jnp.log(k_f) - jnp.log(fhi_f + 0.5)) * st["whi"]
        frac = jnp.clip(glo / (glo + ghi), 0.005, 0.995)
        cand = lo + (hi - lo) * frac
        klo, khi, kc = to_key(lo), to_key(hi), to_key(cand)
        adjacent = klo + 1 >= khi
        mid = (lax.shift_right_arithmetic(klo, 1) + lax.shift_right_arithmetic(khi, 1) + (klo & khi & 1))
        outside = jnp.logical_or(jnp.logical_or(kc <= klo, kc >= khi), (st["it"] & 3) == 3)
        cand = jnp.where(jnp.logical_and(outside, jnp.logical_not(adjacent)), from_key(mid), cand)
        (cnt,) = count(lambda s, r0: s >= cand)
        active = st["done"] == 0
        up = jnp.logical_and(active, cnt > n_sel)
        dn = jnp.logical_and(active, cnt < n_sel)
        hit = jnp.logical_and(active, cnt == n_sel)
        stuck = jnp.where(adjacent, st["stuck"] + 1, 0)
        tied = jnp.logical_and(jnp.logical_and(active, jnp.logical_not(hit)), stuck > SUB_ULP_STEPS)
        last = st["last"]
        new_lo, new_fhi = jnp.where(up, cand, lo), jnp.where(dn, cnt, fhi)
        return dict(
            it=st["it"] + 1,
            lo=new_lo, flo=jnp.where(up, cnt, flo),
            hi=jnp.where(dn, cand, hi), fhi=new_fhi,
            wlo=jnp.where(dn, jnp.where(last == -1, st["wlo"] * 0.5, 1.0), jnp.where(up, 1.0, st["wlo"])),
            whi=jnp.where(up, jnp.where(last == 1, st["whi"] * 0.5, 1.0), jnp.where(dn, 1.0, st["whi"])),
            last=jnp.where(up, 1, jnp.where(dn, -1, last)),
            stuck=stuck,
            done=jnp.where(jnp.logical_or(hit, tied), 1, st["done"]),
            thr=jnp.where(hit, cand, jnp.where(tied, new_lo, st["thr"])),
            tie=jnp.where(tied, 1, st["tie"]),
            need=jnp.where(tied, n_sel - new_fhi, st["need"]))

    state = lax.while_loop(searching, search_step, state)
    thr, tie, need = state["thr"], state["tie"], state["need"]

    def tie_cut():
        def step(t, c0):
            cand = c0 | lax.shift_left(jnp.int32(1), idx_bits - 1 - t)
            (cnt,) = count(lambda s, r0: jnp.where(s == thr, row_iota2 + r0, cand) < cand)
            return jnp.where(cnt < need, cand, c0)
        return lax.fori_loop(0, idx_bits, step, jnp.zeros((1, tq), I32))

    any_tie = jnp.max(tie) > 0
    cut = lax.cond(any_tie, tie_cut, lambda: jnp.zeros((1, tq), I32))
    cut = jnp.where(jnp.logical_or(take_all, zero_cut), -1, jnp.where(tie > 0, cut, jnp.int32(2 ** 30)))

    acc_ref[...] = jnp.zeros(acc_ref.shape, F32)
    ones_rows = jnp.ones((BF16_ROWS, tk), BF16)

    def masked_scores(j):
        r0 = pl.multiple_of(j * tk, tk)
        sc = sc_ref[pl.ds(r0, tk), :]
        bias = jnp.where(sc > thr, 0.0,
                         jnp.where(sc == thr, jnp.where(row_iota + r0 <= cut, 0.0, MASKED), MASKED))
        bias = jnp.concatenate([bias] * GROUP_A, axis=1)
        k = ka_ref[pl.ds(r0, tk), :]
        return [_dot(k, qa[g]) + bias for g in range(N_KV_A)]

    def values(g, j):
        v = va_ref[g * HEAD_DIM:(g + 1) * HEAD_DIM, pl.ds(pl.multiple_of(j * tk, tk), tk)]
        return jnp.concatenate([v, ones_rows], axis=0)

    def attend_bounded():
        def tiles(js, _):
            pv = [0.0] * N_KV_A
            for c in range(SUPER):
                j = js * SUPER + c
                for g, s in enumerate(masked_scores(j)):
                    pv[g] = pv[g] + _dot(values(g, j), jnp.exp2(s).astype(BF16))
            for g in range(N_KV_A):
                acc_ref[g] += pv[g]
            return 0
        lax.fori_loop(0, n_super, tiles, 0)

    def attend_online():
        def tile(j, ms):
            new_ms = []
            for g, s in enumerate(masked_scores(j)):
                m_new = jnp.maximum(ms[g], jnp.max(s, axis=0, keepdims=True))
                p = jnp.exp2(s - m_new).astype(BF16)
                acc_ref[g] = acc_ref[g] * jnp.exp2(ms[g] - m_new) + _dot(values(g, j), p)
                new_ms.append(m_new)
            return tuple(new_ms)
        m0 = jnp.full((1, GROUP_A * tq), MASKED, F32)
        lax.fori_loop(0, n_tiles, tile, (m0,) * N_KV_A)

    @pl.when(i == 0)
    def _():
        kmax_ref[...] = jnp.broadcast_to(_max_head_norms(ka_ref, ones_ref, SUPER * tk), kmax_ref.shape)

    score_bound = jnp.float32(0.0)
    for g in range(N_KV_A):
        qf = qa[g].astype(F32)
        qn = jnp.sqrt(jnp.sum(qf * qf, axis=0, keepdims=True))
        score_bound = jnp.maximum(score_bound, jnp.max(qn * kmax_ref[0:1, g * HEAD_DIM:g * HEAD_DIM + 1]))
    lax.cond(score_bound <= SAFE_EXP2, attend_bounded, attend_online)
    for g in range(N_KV_A):
        out = acc_ref[g, :HEAD_DIM] / acc_ref[g, HEAD_DIM:HEAD_DIM + 1]
        for hh in range(GROUP_A):
            r = (g * GROUP_A + hh) * HEAD_DIM
            o_ref[r:r + HEAD_DIM, :] = out[:, hh * tq:(hh + 1) * tq].astype(o_ref.dtype)


def _dsa(qi_t, w_t, qa_t, kiwi, ka, va_t, *, pos0, n_keys, n_sel, tq, tk):
    b, _, t = qa_t.shape
    lp = ka.shape[1]
    kern = functools.partial(_dsa_kernel, pos0=pos0, n_keys=n_keys, n_sel=n_sel, tq=tq, tk=tk,
                             idx_bits=int(np.ceil(np.log2(lp))))
    once = pl.Buffered(1)
    per_q = lambda rows: pl.BlockSpec((None, rows, tq), lambda bi, i: (bi, 0, i))
    per_b = lambda s: pl.BlockSpec((None,) + s, lambda bi, i: (bi, 0, 0), pipeline_mode=once)
    return pl.pallas_call(
        kern,
        grid=(b, t // tq),
        in_specs=[per_q(WIDTH_I), per_q(N_IDX_HEADS), per_q(WIDTH_A),
                  per_b((lp, LANES)), per_b((lp, KV_A)), per_b((KV_A, lp)), _const_spec((KV_A, KV_A))],
        out_specs=per_q(WIDTH_A),
        out_shape=jax.ShapeDtypeStruct((b, WIDTH_A, t), BF16),
        scratch_shapes=[pltpu.VMEM((lp, tq), F32),
                        pltpu.VMEM((N_KV_A, HEAD_DIM + BF16_ROWS, GROUP_A * tq), F32),
                        pltpu.VMEM((SUBLANES, KV_A), F32)],
        compiler_params=pltpu.CompilerParams(dimension_semantics=("arbitrary", "arbitrary"),
                                             vmem_limit_bytes=VMEM_LIMIT_BYTES),
        name="dsa",
    )(qi_t, w_t, qa_t, kiwi, ka, va_t, _block_ones(N_KV_A))


def _stick_kernel(q_ref, k_ref, v_ref, tri_ref, ones_ref, o_ref, acc_ref, kmax_ref,
                  *, pos0, tq, tk, heads):
    i = pl.program_id(2)

    @pl.when(i == 0)
    def _():
        kmax_ref[...] = jnp.broadcast_to(_max_head_norms(k_ref, ones_ref, SUPER * tk), kmax_ref.shape)

    blank = jnp.zeros((HEAD_DIM, tq), BF16)
    q, bounds = [], []
    for h in range(heads):
        qh = q_ref[h * HEAD_DIM:(h + 1) * HEAD_DIM, :]
        parts = [blank] * heads
        parts[h] = qh
        q.append(jnp.concatenate(parts, axis=0))
        qf = qh.astype(F32)
        qn = jnp.sqrt(jnp.sum(qf * qf, axis=0, keepdims=True))
        bounds.append(qn * kmax_ref[0:1, h * HEAD_DIM:h * HEAD_DIM + 1] * STICK_BOUND_SLACK)
    pos_first = pos0 + i * tq
    pos_row = pos_first + lax.broadcasted_iota(I32, (1, tq), 1)
    row_iota = lax.broadcasted_iota(I32, (tk, tq), 0)
    j_first = jnp.maximum(pos_first + tq - 2, 0) // tk
    tri = tri_ref[...]
    acc_ref[...] = jnp.zeros(acc_ref.shape, F32)

    def cond(state):
        j, live, _ = state
        return jnp.logical_and(j >= 0, live > 0)

    def body(state):
        j, _, carries = state
        r0 = pl.multiple_of(j * tk, tk)
        vis = row_iota + r0 < pos_row
        hs = range(heads)
        k = k_ref[pl.ds(r0, tk), :]
        z = [_dot(k, q[h]) for h in hs]
        softplus = [jnp.maximum(z[h], 0.0) + jnp.log2(1.0 + jnp.exp2(-jnp.abs(z[h]))) for h in hs]
        log_keep = [jnp.where(vis, -softplus[h], 0.0) for h in hs]
        incl = [_dot_hilo_left(tri, log_keep[h]) for h in hs]
        a = [jnp.where(vis, jnp.exp2(z[h] + incl[h] + carries[h]), 0.0).astype(BF16) for h in hs]
        pv = [_dot(v_ref[h * HEAD_DIM:(h + 1) * HEAD_DIM, pl.ds(r0, tk)], a[h]) for h in hs]
        for h in hs:
            acc_ref[h] += pv[h]
        new_carries = [carries[h] + incl[h][0:1, :] for h in hs]
        worst = bounds[0] + new_carries[0]
        for h in range(1, heads):
            worst = jnp.maximum(worst, bounds[h] + new_carries[h])
        live = (jnp.max(worst) >= STICK_CUTOFF).astype(I32)
        return j - 1, live, tuple(new_carries)

    init = (j_first, jnp.int32(1), (jnp.zeros((1, tq), F32),) * heads)
    lax.while_loop(cond, body, init)
    for h in range(heads):
        o_ref[h * HEAD_DIM:(h + 1) * HEAD_DIM, :] = acc_ref[h].astype(o_ref.dtype)


def _stick(q_t, k, v_t, *, pos0, tq, tk):
    b, width, t = q_t.shape
    lp = k.shape[1]
    hb = STICK_HEADS
    wb = hb * HEAD_DIM
    assert width % wb == 0
    tri = jnp.asarray(np.triu(np.ones((tk, tk), np.float32)), BF16)
    kern = functools.partial(_stick_kernel, pos0=pos0, tq=tq, tk=tk, heads=hb)
    once = pl.Buffered(1)
    return pl.pallas_call(
        kern,
        grid=(b, width // wb, t // tq),
        in_specs=[pl.BlockSpec((None, wb, tq), lambda bi, h, i: (bi, h, i)),
                  pl.BlockSpec((None, lp, wb), lambda bi, h, i: (bi, 0, h), pipeline_mode=once),
                  pl.BlockSpec((None, wb, lp), lambda bi, h, i: (bi, h, 0), pipeline_mode=once),
                  _const_spec((tk, tk)), _const_spec((wb, wb))],
        out_specs=pl.BlockSpec((None, wb, tq), lambda bi, h, i: (bi, h, i)),
        out_shape=jax.ShapeDtypeStruct((b, width, t), BF16),
        scratch_shapes=[pltpu.VMEM((hb, HEAD_DIM, tq), F32), pltpu.VMEM((SUBLANES, wb), F32)],
        compiler_params=pltpu.CompilerParams(dimension_semantics=("arbitrary",) * 3,
                                             vmem_limit_bytes=VMEM_LIMIT_BYTES),
        name="stick",
    )(q_t, k, v_t, tri, _block_ones(hb))


def _rms(x, g):
    ms = jnp.mean(x * x, axis=-1, keepdims=True)
    return (x * lax.rsqrt(ms + EPS)) * g


def _tail_kernel(x_ref, oa_ref, ob_ref, ga_ref, gb_ref, p_ref, pa_ref, pb_ref, wo_ref,
                 nf_ref, wu_ref, wd_ref, np_ref, wp_ref, wg_ref, y_ref, *, ff_chunk, feature_major):
    def rows(ref):
        return ref[...].astype(F32).T.astype(BF16) if feature_major else ref[...]

    mix = (jax.nn.sigmoid(ga_ref[...]) * _dot(rows(oa_ref), pa_ref[...])
           + jax.nn.sigmoid(gb_ref[...]) * _dot(rows(ob_ref), pb_ref[...]))
    x = x_ref[...] + _dot(mix.astype(BF16), wo_ref[...])
    h = _rms(x, nf_ref[...]).astype(BF16)
    d_ff = wu_ref.shape[1]
    ffn = jnp.zeros_like(x)
    for c in range(d_ff // ff_chunk):
        u = jnp.maximum(_dot(h, wu_ref[:, c * ff_chunk:(c + 1) * ff_chunk]), 0.0)
        ffn = ffn + _dot((u * u).astype(BF16), wd_ref[c * ff_chunk:(c + 1) * ff_chunk, :])
    x = x + ffn
    h = _rms(x, np_ref[...]).astype(BF16)
    y_ref[...] = x + _dot(p_ref[...].astype(BF16), wp_ref[...]) * jax.nn.sigmoid(_dot(h, wg_ref[...]))


def _tail(x, o_a, o_b, ga, gb, p, proj_a, proj_b, w_out, norm_ffn, w_up, w_down,
          norm_ple, w_ple, w_gate):
    b, t, d_model = x.shape
    n = b * t
    tm = min(TOKEN_TILE, n)
    assert n % tm == 0
    feature_major = t % tm == 0
    d_ff = w_up.shape[1]
    row = lambda w: pl.BlockSpec((tm, w), lambda i: (i, 0))
    if feature_major:
        tiles_per_seq = t // tm
        branch = lambda w: pl.BlockSpec((None, w, tm), lambda i: (i // tiles_per_seq, 0, i % tiles_per_seq))
    else:
        o_a, o_b = (o.transpose(0, 2, 1).reshape(n, o.shape[1]) for o in (o_a, o_b))
        branch = row
    weights = (proj_a, proj_b, w_out, norm_ffn.reshape(1, d_model), w_up, w_down,
               norm_ple.reshape(1, d_model), w_ple, w_gate)
    return pl.pallas_call(
        functools.partial(_tail_kernel, ff_chunk=min(1024, d_ff), feature_major=feature_major),
        grid=(n // tm,),
        in_specs=[row(d_model), branch(WIDTH_A), branch(WIDTH_B), row(d_model), row(d_model),
                  row(p.shape[-1])] + [_const_spec(w.shape) for w in weights],
        out_specs=row(d_model),
        out_shape=jax.ShapeDtypeStruct((n, d_model), F32),
        compiler_params=pltpu.CompilerParams(dimension_semantics=("arbitrary",),
                                             vmem_limit_bytes=VMEM_LIMIT_BYTES),
        name="tail",
    )(x.reshape(n, d_model), o_a, o_b, ga, gb, p.reshape(n, p.shape[-1]), *weights).reshape(b, t, d_model)


def _layer(x, p, pos0, caches, wts):
    (norm_mix, w_in_p, g_qa, g_ka, proj_a, proj_b, w_out, norm_ffn, w_up, w_down,
     norm_ple, w_ple, w_gate) = wts
    b, t, d_model = x.shape
    pos = pos0 + jnp.arange(t, dtype=I32)
    pr = _project(x, pos, norm_mix, w_in_p, g_qa, g_ka)

    per_seq = lambda a: a.reshape(b, t, a.shape[-1])
    new_state = (per_seq(pr["ka"]).reshape(b, t, N_KV_A, HEAD_DIM),
                 per_seq(pr["va"]).reshape(b, t, N_KV_A, HEAD_DIM),
                 per_seq(pr["kiwi"])[:, :, :IDX_DIM],
                 per_seq(pr["kb"]).reshape(b, t, N_HEADS_B, HEAD_DIM),
                 per_seq(pr["vb"]).reshape(b, t, N_HEADS_B, HEAD_DIM))
    kiwi, ka, kb = per_seq(pr["kiwi_bf"]), per_seq(pr["ka_bf"]), per_seq(pr["kb_bf"])
    va_t, vb_t = pr["va_bf"], pr["vb_bf"]
    tk = KEY_TILE
    if caches is not None:
        c_ka, c_va, c_ki, c_kb, c_vb = caches
        past = c_ka.shape[1]
        flat = lambda c: c.reshape(b, past, -1).astype(BF16)
        ki_lanes = jnp.pad(flat(c_ki), ((0, 0), (0, 0), (0, LANES - IDX_DIM)))
        kiwi = jnp.concatenate([ki_lanes, kiwi], axis=1)
        ka = jnp.concatenate([flat(c_ka), ka], axis=1)
        kb = jnp.concatenate([flat(c_kb), kb], axis=1)
        va_t = jnp.concatenate([flat(c_va).transpose(0, 2, 1), va_t], axis=2)
        vb_t = jnp.concatenate([flat(c_vb).transpose(0, 2, 1), vb_t], axis=2)
    n_keys = ka.shape[1]
    lp = -(-n_keys // (SUPER * tk)) * (SUPER * tk)
    if lp != n_keys:
        pad_rows = lambda a: jnp.pad(a, ((0, 0), (0, lp - n_keys), (0, 0)))
        pad_cols = lambda a: jnp.pad(a, ((0, 0), (0, 0), (0, lp - n_keys)))
        kiwi, ka, kb, va_t, vb_t = pad_rows(kiwi), pad_rows(ka), pad_rows(kb), pad_cols(va_t), pad_cols(vb_t)
    n_sel = min(TOPK_MAX, n_keys // 4)

    o_a = _dsa(pr["qi"], pr["w"], pr["qa"], kiwi, ka, va_t, pos0=pos0, n_keys=n_keys, n_sel=n_sel,
               tq=min(Q_TILE_A, t), tk=tk)
    o_b = _stick(pr["qb"], kb, vb_t, pos0=pos0, tq=min(Q_TILE_B, t), tk=tk)
    y = _tail(x, o_a, o_b, pr["ga"], pr["gb"], p, proj_a, proj_b, w_out, norm_ffn,
              w_up, w_down, norm_ple, w_ple, w_gate)
    return y, new_state


def kernel(x_prompt, x_sample, cache_a_k, cache_a_v, cache_a_kidx, cache_b_k, cache_b_v, p_prompt, p_sample, norm_mix, w_in, g_qa, g_ka, proj_a, proj_b, w_out, norm_ffn, w_up, w_down, norm_ple, w_ple, w_ple_gate):
    depth = w_in.shape[0]
    past = cache_a_k.shape[2]
    d_model = x_prompt.shape[-1]
    split = _SEG["kiwi"][0] + IDX_DIM + N_IDX_HEADS
    assert w_in.shape[2] == split + 3 * WIDTH_B + 2 * d_model
    xp, xs = x_prompt, x_sample
    outs_p, outs_s = [], []
    for i in range(depth):
        w_in_p = jnp.concatenate(
            [w_in[i, :, :split], jnp.zeros((d_model, LANES - IDX_DIM - N_IDX_HEADS), w_in.dtype),
             w_in[i, :, split:]], axis=1).astype(BF16)
        wts = (norm_mix[i], w_in_p, g_qa[i], g_ka[i], proj_a[i].astype(BF16), proj_b[i].astype(BF16),
               w_out[i].astype(BF16), norm_ffn[i], w_up[i].astype(BF16), w_down[i].astype(BF16),
               norm_ple[i], w_ple[i].astype(BF16), w_ple_gate[i].astype(BF16))
        xp, st = _layer(xp, p_prompt[i], 0, None, wts)
        outs_p.append(st)
        caches = (cache_a_k[i], cache_a_v[i], cache_a_kidx[i], cache_b_k[i], cache_b_v[i])
        xs, st = _layer(xs, p_sample[i], past, caches, wts)
        outs_s.append(st)
    stack = lambda outs, k: jnp.stack([o[k] for o in outs], 0)
    return ((xp, xs) + tuple(stack(outs_p, k) for k in range(5))
            + tuple(stack(outs_s, k) for k in range(5)))
```

```python
import functools

import numpy as np
import jax
import jax.numpy as jnp
from jax import lax
from jax.experimental import pallas as pl
from jax.experimental.pallas import tpu as pltpu

CHUNK = 64
HEAD_DIM = 64
N_HEADS_A = 8
N_KV_A = 2
GROUP_A = N_HEADS_A // N_KV_A
N_HEADS_B = 8
N_IDX_HEADS = 8
IDX_DIM = 32
TOPK_MAX = 256
ROPE_THETA = 500000.0
EPS = 1e-6
WIDTH_A = N_HEADS_A * HEAD_DIM
WIDTH_B = N_HEADS_B * HEAD_DIM
WIDTH_I = N_IDX_HEADS * IDX_DIM
KV_A = N_KV_A * HEAD_DIM

LANES = 128
SUBLANES = 8
BF16_ROWS = 16
VMEM_LIMIT_BYTES = 56 * 1024 * 1024

TOKEN_TILE = 256
KEY_TILE = 256
Q_TILE_A = 256
Q_TILE_B = 256
SUPER = 4
COUNT_ROWS = 64
STICK_HEADS = 4
LOG2E = 1.4426950408889634
SAFE_EXP2 = 60.0
NORM_SLACK = 1.01
SEARCH_CAP = 200
NARROW_BRACKET = 48
SUB_ULP_STEPS = 6

MASKED = -1e30
STICK_CUTOFF = -120.0 * LOG2E
STICK_BOUND_SLACK = 1.01

F32 = jnp.float32
BF16 = jnp.bfloat16
I32 = jnp.int32


def _const_spec(shape):
    nd = len(shape)
    return pl.BlockSpec(shape, lambda *_: (0,) * nd, pipeline_mode=pl.Buffered(1))


def _dot(a, b):
    return jnp.dot(a, b, preferred_element_type=F32)


def _dot_hilo(x, m):
    hi = x.astype(BF16)
    lo = (x - hi.astype(F32)).astype(BF16)
    return _dot(hi, m) + _dot(lo, m)


def _dot_hilo_left(m, x):
    hi = x.astype(BF16)
    lo = (x - hi.astype(F32)).astype(BF16)
    return _dot(m, hi) + _dot(m, lo)


def _block_ones(n_blocks, value=1.0):
    return jnp.asarray(np.kron(np.eye(n_blocks, dtype=np.float32),
                               np.full((HEAD_DIM, HEAD_DIM), value, np.float32)), BF16)


def _max_head_norms(k_ref, ones_ref, chunk):
    ones = ones_ref[...]

    def body(j, mx):
        kf = k_ref[pl.ds(pl.multiple_of(j * chunk, chunk), chunk), :].astype(F32)
        sq = _dot((kf * kf).astype(BF16), ones)
        part = sq[0:SUBLANES]
        for r in range(1, chunk // SUBLANES):
            part = jnp.maximum(part, sq[r * SUBLANES:(r + 1) * SUBLANES])
        return jnp.maximum(mx, part)
    mx = lax.fori_loop(0, k_ref.shape[0] // chunk, body, jnp.zeros((SUBLANES, k_ref.shape[1]), F32))
    return jnp.sqrt(jnp.max(mx, axis=0, keepdims=True) * NORM_SLACK)


_SEG = {}
_off = 0
for _name, _w in (("qa", WIDTH_A), ("ka", KV_A), ("va", KV_A), ("qi", WIDTH_I),
                  ("kiwi", LANES), ("qb", WIDTH_B), ("kb", WIDTH_B), ("vb", WIDTH_B), ("gates", 0)):
    _SEG[_name] = (_off, _w)
    _off += _w
del _off, _name, _w
IDX_WEIGHT_SCALE = (N_IDX_HEADS ** -0.5) * (IDX_DIM ** -0.5)
Q_SCALE = HEAD_DIM ** -0.5 * LOG2E


def _rope(y, c, s1, s2, half):
    cols = []
    for k in range(y.shape[1] // LANES):
        yk = y[:, k * LANES:(k + 1) * LANES]
        cols.append(yk * c + pltpu.roll(yk, LANES - half, 1) * s1 + pltpu.roll(yk, half, 1) * s2)
    return cols[0] if len(cols) == 1 else jnp.concatenate(cols, axis=1)


def _proj_kernel(x_ref, g_ref, w_ref, gq_ref, gk_ref, gm_ref, tab_ref,
                 ka_ref, va_ref, kiwi_ref, kb_ref, vb_ref, ga_ref, gb_ref,
                 kab_ref, kiwib_ref, kbb_ref, qa_ref, qi_ref, qb_ref, w_out_ref, vab_ref, vbb_ref,
                 *, d_model, feature_major):
    x = x_ref[...]
    ms = jnp.mean(x * x, axis=-1, keepdims=True)
    h = ((x * lax.rsqrt(ms + EPS)) * g_ref[...]).astype(BF16)

    def seg(name):
        off, w = _SEG[name]
        return _dot(h, w_ref[:, off:off + w])

    def tab(k):
        return tab_ref[:, k * LANES:(k + 1) * LANES]

    def head_norm(u, gain):
        w = u.shape[1]
        msq = _dot_hilo(u * u, gm_ref[:w, :w])
        return (u * lax.rsqrt(msq + EPS)) * gain

    def put(ref, val):
        ref[...] = (val.T if feature_major else val).astype(ref.dtype)

    qa = head_norm(seg("qa"), gq_ref[...])
    put(qa_ref, _rope(qa, tab(0), tab(1), tab(2), HEAD_DIM // 8) * Q_SCALE)
    ka = _rope(head_norm(seg("ka"), gk_ref[...]), tab(0), tab(1), tab(2), HEAD_DIM // 8)
    ka_ref[...] = ka
    kab_ref[...] = ka.astype(BF16)
    va = seg("va")
    va_ref[...] = va
    put(vab_ref, va)
    put(qi_ref, _rope(seg("qi"), tab(3), tab(4), tab(5), IDX_DIM // 8))
    kiwi = _rope(seg("kiwi"), tab(6), tab(7), tab(8), IDX_DIM // 8)
    kiwi_ref[...] = kiwi
    kiwib_ref[...] = kiwi.astype(BF16)
    wi = kiwi * IDX_WEIGHT_SCALE
    if feature_major:
        w_out_ref[...] = wi.T[IDX_DIM:IDX_DIM + N_IDX_HEADS, :]
    else:
        w_out_ref[...] = wi
    put(qb_ref, seg("qb") * Q_SCALE)
    kb = seg("kb")
    kb_ref[...] = kb
    kbb_ref[...] = kb.astype(BF16)
    vb = seg("vb")
    vb_ref[...] = vb
    put(vbb_ref, vb)
    g_off = _SEG["gates"][0]
    ga_ref[...] = _dot(h, w_ref[:, g_off:g_off + d_model])
    gb_ref[...] = _dot(h, w_ref[:, g_off + d_model:g_off + 2 * d_model])


def _rope_tables(pos):
    def one(d):
        r = d // 4
        hr = r // 2
        inv = jnp.power(jnp.float32(ROPE_THETA), -jnp.arange(hr, dtype=F32) * (2.0 / r))
        ang = pos.astype(F32)[:, None] * inv[None, :]
        cos, sin = jnp.cos(ang), jnp.sin(ang)
        t = pos.shape[0]
        one_ = jnp.ones((t, d - r), F32)
        zero = jnp.zeros((t, d - r), F32)
        zh = jnp.zeros((t, hr), F32)
        c = jnp.concatenate([cos, cos, one_], axis=1)
        s1 = jnp.concatenate([-sin, zh, zero], axis=1)
        s2 = jnp.concatenate([zh, sin, zero], axis=1)
        return c, s1, s2

    c64, s164, s264 = one(HEAD_DIM)
    c32, s132, s232 = one(IDX_DIM)
    t = pos.shape[0]
    rep = lambda a, d: jnp.tile(a, (1, LANES // d))
    padc = jnp.ones((t, LANES - IDX_DIM), F32)
    padz = jnp.zeros((t, LANES - IDX_DIM), F32)
    return jnp.concatenate([
        rep(c64, HEAD_DIM), rep(s164, HEAD_DIM), rep(s264, HEAD_DIM),
        rep(c32, IDX_DIM), rep(s132, IDX_DIM), rep(s232, IDX_DIM),
        jnp.concatenate([c32, padc], axis=1), jnp.concatenate([s132, padz], axis=1),
        jnp.concatenate([s232, padz], axis=1)], axis=1)


def _project(x, pos, norm_mix, w_in_p, g_qa, g_ka):
    b, t, d_model = x.shape
    n = b * t
    tm = min(TOKEN_TILE, n)
    assert n % tm == 0 and (t % tm == 0 or tm % t == 0)
    feature_major = t % tm == 0
    tables = _rope_tables(pos)
    if t < tm:
        tables = jnp.tile(tables, (tm // t, 1))
    tiles_per_seq = max(t // tm, 1)
    gm = _block_ones(N_HEADS_A, 1.0 / HEAD_DIM)
    gq = jnp.tile(g_qa.reshape(1, HEAD_DIM), (1, N_HEADS_A))
    gk = jnp.tile(g_ka.reshape(1, HEAD_DIM), (1, N_KV_A))

    row = lambda w: pl.BlockSpec((tm, w), lambda i: (i, 0))
    rows = lambda w, dt: (jax.ShapeDtypeStruct((n, w), dt), row(w))
    if feature_major:
        cols = lambda w, dt: (jax.ShapeDtypeStruct((b, w, t), dt),
                              pl.BlockSpec((None, w, tm), lambda i: (i // tiles_per_seq, 0, i % tiles_per_seq)))
        w_out = cols(N_IDX_HEADS, F32)
    else:
        cols = rows
        w_out = rows(LANES, F32)
    names = ["ka", "va", "kiwi", "kb", "vb", "ga", "gb", "ka_bf", "kiwi_bf", "kb_bf",
             "qa", "qi", "qb", "w", "va_bf", "vb_bf"]
    outs = [rows(KV_A, F32), rows(KV_A, F32), rows(LANES, F32), rows(WIDTH_B, F32), rows(WIDTH_B, F32),
            rows(d_model, F32), rows(d_model, F32), rows(KV_A, BF16), rows(LANES, BF16), rows(WIDTH_B, BF16),
            cols(WIDTH_A, BF16), cols(WIDTH_I, BF16), cols(WIDTH_B, BF16), w_out,
            cols(KV_A, BF16), cols(WIDTH_B, BF16)]
    res = pl.pallas_call(
        functools.partial(_proj_kernel, d_model=d_model, feature_major=feature_major),
        grid=(n // tm,),
        in_specs=[row(d_model), _const_spec((1, d_model)), _const_spec(w_in_p.shape),
                  _const_spec((1, WIDTH_A)), _const_spec((1, KV_A)), _const_spec((WIDTH_A, WIDTH_A)),
                  pl.BlockSpec((tm, 9 * LANES), lambda i: (i % tiles_per_seq, 0))],
        out_specs=[o[1] for o in outs],
        out_shape=[o[0] for o in outs],
        compiler_params=pltpu.CompilerParams(dimension_semantics=("arbitrary",),
                                             vmem_limit_bytes=VMEM_LIMIT_BYTES),
        name="proj",
    )(x.reshape(n, d_model), norm_mix.reshape(1, d_model), w_in_p, gq, gk, gm, tables)
    p = dict(zip(names, res))
    if not feature_major:
        to_cols = lambda a: a.reshape(b, t, a.shape[-1]).transpose(0, 2, 1)
        for k in ("qa", "qi", "qb", "va_bf", "vb_bf"):
            p[k] = to_cols(p[k])
        p["w"] = to_cols(p["w"][:, IDX_DIM:IDX_DIM + N_IDX_HEADS])
    return p


def _dsa_kernel(qi_ref, w_ref, qa_ref, ki_ref, ka_ref, va_ref, ones_ref, o_ref,
                sc_ref, acc_ref, kmax_ref, *, pos0, n_keys, n_sel, tq, tk, idx_bits):
    i = pl.program_id(1)
    pos_first = pos0 + i * tq
    lim_max = jnp.minimum(((pos_first + tq - 1) // CHUNK + 1) * CHUNK, n_keys)
    n_tiles = (lim_max + tk - 1) // tk
    lane_pos = pos_first + lax.broadcasted_iota(I32, (1, tq), 1)
    lim_row = jnp.minimum((lax.shift_right_arithmetic(lane_pos, int(np.log2(CHUNK))) + 1) * CHUNK, n_keys)
    neg_inf = jnp.float32(-jnp.inf)

    n_super = (n_tiles + SUPER - 1) // SUPER
    row_iota = lax.broadcasted_iota(I32, (tk, tq), 0)

    def lanes_by_head(ref, first, n_heads, width):
        return jnp.concatenate([ref[(first + h) * width:(first + h + 1) * width, :] for h in range(n_heads)],
                               axis=1)

    qi = lanes_by_head(qi_ref, 0, N_IDX_HEADS, IDX_DIM)
    qi = jnp.concatenate([qi, jnp.zeros((LANES - IDX_DIM, N_IDX_HEADS * tq), BF16)], axis=0)
    blank = jnp.zeros((HEAD_DIM, GROUP_A * tq), BF16)
    qa = []
    for g in range(N_KV_A):
        parts = [blank] * N_KV_A
        parts[g] = lanes_by_head(qa_ref, g * GROUP_A, GROUP_A, HEAD_DIM)
        qa.append(jnp.concatenate(parts, axis=0))
    w = w_ref[...]

    def fold_rows(x, op):
        out = x[0:SUBLANES]
        for r in range(1, x.shape[0] // SUBLANES):
            out = op(out, x[r * SUBLANES:(r + 1) * SUBLANES])
        return out

    def score_tiles(js, carry):
        smax, smin = carry
        for c in range(SUPER):
            r0 = pl.multiple_of((js * SUPER + c) * tk, tk)
            lg = _dot(ki_ref[pl.ds(r0, tk), :], qi)
            sc = jnp.zeros((tk, tq), F32)
            for h in range(N_IDX_HEADS):
                sc = sc + jnp.maximum(lg[:, h * tq:(h + 1) * tq], 0.0) * w[h:h + 1, :]
            adm = row_iota + r0 < lim_row
            masked = jnp.where(adm, sc, neg_inf)
            sc_ref[pl.ds(r0, tk), :] = masked
            smax = jnp.maximum(smax, fold_rows(masked, jnp.maximum))
            smin = jnp.minimum(smin, fold_rows(jnp.where(adm, sc, -neg_inf), jnp.minimum))
        return smax, smin

    smax, smin = lax.fori_loop(0, n_super, score_tiles,
                               (jnp.full((SUBLANES, tq), neg_inf, F32), jnp.full((SUBLANES, tq), -neg_inf, F32)))
    smax = jnp.max(smax, axis=0, keepdims=True)
    smin = jnp.min(smin, axis=0, keepdims=True)

    n_steps = (n_tiles + 1) // 2
    tk2 = 2 * tk
    row_iota2 = lax.broadcasted_iota(I32, (tk2, tq), 0)

    def count(*preds):
        def body(j, accs):
            r0 = pl.multiple_of(j * tk2, tk2)
            s = sc_ref[pl.ds(r0, tk2), :]
            out = []
            for pred, acc in zip(preds, accs):
                hit = jnp.where(pred(s, r0), 1, 0).astype(I32)
                for r in range(tk2 // COUNT_ROWS):
                    acc = acc + hit[r * COUNT_ROWS:(r + 1) * COUNT_ROWS]
                out.append(acc)
            return tuple(out)
        accs = lax.fori_loop(0, n_steps, body, (jnp.zeros((COUNT_ROWS, tq), I32),) * len(preds))
        return [jnp.sum(a, axis=0, keepdims=True) for a in accs]

    def to_key(x):
        bits = pltpu.bitcast(x, I32)
        return bits ^ (lax.shift_right_arithmetic(bits, 31) & 0x7FFFFFFF)

    def from_key(k):
        return pltpu.bitcast(k ^ (lax.shift_right_arithmetic(k, 31) & 0x7FFFFFFF), F32)

    k_f = jnp.float32(n_sel)
    n_adm = lim_row
    zero = jnp.zeros((1, tq), F32)
    fge0, fgt0 = count(lambda s, r0: s >= zero, lambda s, r0: s > zero)
    take_all = n_adm <= n_sel
    zero_cut = fgt0 == n_sel
    zero_tie = jnp.logical_and(fgt0 < n_sel, fge0 >= n_sel)
    pos_side = fgt0 > n_sel
    neg_side = fge0 < n_sel
    as_i = lambda m: jnp.where(m, 1, 0).astype(I32)
    state = dict(
        it=jnp.int32(0),
        lo=jnp.where(pos_side, 0.0, smin), flo=jnp.where(pos_side, fge0, n_adm),
        hi=jnp.where(neg_side, 0.0, from_key(to_key(smax) + 1)), fhi=jnp.where(neg_side, fge0, 0),
        wlo=jnp.ones((1, tq), F32), whi=jnp.ones((1, tq), F32), last=jnp.zeros((1, tq), I32),
        stuck=jnp.zeros((1, tq), I32),
        done=as_i(jnp.logical_or(take_all, jnp.logical_or(zero_cut, zero_tie))),
        thr=jnp.where(take_all, neg_inf, 0.0),
        tie=as_i(jnp.logical_and(zero_tie, jnp.logical_not(take_all))),
        need=n_sel - fgt0)

    def searching(st):
        return jnp.logical_and(jnp.min(st["done"]) == 0, st["it"] < SEARCH_CAP)

    def search_step(st):
        lo, hi, flo, fhi = st["lo"], st["hi"], st["flo"], st["fhi"]
        flo_f, fhi_f = flo.astype(F32), fhi.astype(F32)
        narrow = flo - fhi <= NARROW_BRACKET
        glo = jnp.where(narrow, flo_f - k_f, jnp.log(flo_f + 0.5) - jnp.log(k_f)) * st["wlo"]
        ghi = jnp.where(narrow, k_f - fhi_f, jnp.log(k_f) - jnp.log(fhi_f + 0.5)) * st["whi"]
        frac = jnp.clip(glo / (glo + ghi), 0.005, 0.995)
        cand = lo + (hi - lo) * frac
        klo, khi, kc = to_key(lo), to_key(hi), to_key(cand)
        adjacent = klo + 1 >= khi
        mid = (lax.shift_right_arithmetic(klo, 1) + lax.shift_right_arithmetic(khi, 1) + (klo & khi & 1))
        outside = jnp.logical_or(jnp.logical_or(kc <= klo, kc >= khi), (st["it"] & 3) == 3)
        cand = jnp.where(jnp.logical_and(outside, jnp.logical_not(adjacent)), from_key(mid), cand)
        (cnt,) = count(lambda s, r0: s >= cand)
        active = st["done"] == 0
        up = jnp.logical_and(active, cnt > n_sel)
        dn = jnp.logical_and(active, cnt < n_sel)
        hit = jnp.logical_and(active, cnt == n_sel)
        stuck = jnp.where(adjacent, st["stuck"] + 1, 0)
        tied = jnp.logical_and(jnp.logical_and(active, jnp.logical_not(hit)), stuck > SUB_ULP_STEPS)
        last = st["last"]
        new_lo, new_fhi = jnp.where(up, cand, lo), jnp.where(dn, cnt, fhi)
        return dict(
            it=st["it"] + 1,
            lo=new_lo, flo=jnp.where(up, cnt, flo),
            hi=jnp.where(dn, cand, hi), fhi=new_fhi,
            wlo=jnp.where(dn, jnp.where(last == -1, st["wlo"] * 0.5, 1.0), jnp.where(up, 1.0, st["wlo"])),
            whi=jnp.where(up, jnp.where(last == 1, st["whi"] * 0.5, 1.0), jnp.where(dn, 1.0, st["whi"])),
            last=jnp.where(up, 1, jnp.where(dn, -1, last)),
            stuck=stuck,
            done=jnp.where(jnp.logical_or(hit, tied), 1, st["done"]),
            thr=jnp.where(hit, cand, jnp.where(tied, new_lo, st["thr"])),
            tie=jnp.where(tied, 1, st["tie"]),
            need=jnp.where(tied, n_sel - new_fhi, st["need"]))

    state = lax.while_loop(searching, search_step, state)
    thr, tie, need = state["thr"], state["tie"], state["need"]

    def tie_cut():
        def step(t, c0):
            cand = c0 | lax.shift_left(jnp.int32(1), idx_bits - 1 - t)
            (cnt,) = count(lambda s, r0: jnp.where(s == thr, row_iota2 + r0, cand) < cand)
            return jnp.where(cnt < need, cand, c0)
        return lax.fori_loop(0, idx_bits, step, jnp.zeros((1, tq), I32))

    any_tie = jnp.max(tie) > 0
    cut = lax.cond(any_tie, tie_cut, lambda: jnp.zeros((1, tq), I32))
    cut = jnp.where(jnp.logical_or(take_all, zero_cut), -1, jnp.where(tie > 0, cut, jnp.int32(2 ** 30)))

    acc_ref[...] = jnp.zeros(acc_ref.shape, F32)
    ones_rows = jnp.ones((BF16_ROWS, tk), BF16)

    def masked_scores(j):
        r0 = pl.multiple_of(j * tk, tk)
        sc = sc_ref[pl.ds(r0, tk), :]
        bias = jnp.where(sc > thr, 0.0,
                         jnp.where(sc == thr, jnp.where(row_iota + r0 <= cut, 0.0, MASKED), MASKED))
        bias = jnp.concatenate([bias] * GROUP_A, axis=1)
        k = ka_ref[pl.ds(r0, tk), :]
        return [_dot(k, qa[g]) + bias for g in range(N_KV_A)]

    def values(g, j):
        v = va_ref[g * HEAD_DIM:(g + 1) * HEAD_DIM, pl.ds(pl.multiple_of(j * tk, tk), tk)]
        return jnp.concatenate([v, ones_rows], axis=0)

    def attend_bounded():
        def tiles(js, _):
            js4 = [js * SUPER + c for c in range(SUPER)]
            scores = [masked_scores(j) for j in js4]
            probs = [[jnp.exp2(s).astype(BF16) for s in per_group] for per_group in scores]
            for g in range(N_KV_A):
                pv = _dot(values(g, js4[0]), probs[0][g])
                for c in range(1, SUPER):
                    pv = pv + _dot(values(g, js4[c]), probs[c][g])
                acc_ref[g] += pv
            return 0
        lax.fori_loop(0, n_super, tiles, 0)

    def attend_online():
        def tile(j, ms):
            new_ms = []
            for g, s in enumerate(masked_scores(j)):
                m_new = jnp.maximum(ms[g], jnp.max(s, axis=0, keepdims=True))
                p = jnp.exp2(s - m_new).astype(BF16)
                acc_ref[g] = acc_ref[g] * jnp.exp2(ms[g] - m_new) + _dot(values(g, j), p)
                new_ms.append(m_new)
            return tuple(new_ms)
        m0 = jnp.full((1, GROUP_A * tq), MASKED, F32)
        lax.fori_loop(0, n_tiles, tile, (m0,) * N_KV_A)

    @pl.when(i == 0)
    def _():
        kmax_ref[...] = jnp.broadcast_to(_max_head_norms(ka_ref, ones_ref, SUPER * tk), kmax_ref.shape)

    score_bound = jnp.float32(0.0)
    for g in range(N_KV_A):
        qf = qa[g].astype(F32)
        qn = jnp.sqrt(jnp.sum(qf * qf, axis=0, keepdims=True))
        score_bound = jnp.maximum(score_bound, jnp.max(qn * kmax_ref[0:1, g * HEAD_DIM:g * HEAD_DIM + 1]))
    lax.cond(score_bound <= SAFE_EXP2, attend_bounded, attend_online)
    for g in range(N_KV_A):
        out = acc_ref[g, :HEAD_DIM] / acc_ref[g, HEAD_DIM:HEAD_DIM + 1]
        for hh in range(GROUP_A):
            r = (g * GROUP_A + hh) * HEAD_DIM
            o_ref[r:r + HEAD_DIM, :] = out[:, hh * tq:(hh + 1) * tq].astype(o_ref.dtype)


def _dsa(qi_t, w_t, qa_t, kiwi, ka, va_t, *, pos0, n_keys, n_sel, tq, tk):
    b, _, t = qa_t.shape
    lp = ka.shape[1]
    kern = functools.partial(_dsa_kernel, pos0=pos0, n_keys=n_keys, n_sel=n_sel, tq=tq, tk=tk,
                             idx_bits=int(np.ceil(np.log2(lp))))
    once = pl.Buffered(1)
    per_q = lambda rows: pl.BlockSpec((None, rows, tq), lambda bi, i: (bi, 0, i))
    per_b = lambda s: pl.BlockSpec((None,) + s, lambda bi, i: (bi, 0, 0), pipeline_mode=once)
    return pl.pallas_call(
        kern,
        grid=(b, t // tq),
        in_specs=[per_q(WIDTH_I), per_q(N_IDX_HEADS), per_q(WIDTH_A),
                  per_b((lp, LANES)), per_b((lp, KV_A)), per_b((KV_A, lp)), _const_spec((KV_A, KV_A))],
        out_specs=per_q(WIDTH_A),
        out_shape=jax.ShapeDtypeStruct((b, WIDTH_A, t), BF16),
        scratch_shapes=[pltpu.VMEM((lp, tq), F32),
                        pltpu.VMEM((N_KV_A, HEAD_DIM + BF16_ROWS, GROUP_A * tq), F32),
                        pltpu.VMEM((SUBLANES, KV_A), F32)],
        compiler_params=pltpu.CompilerParams(dimension_semantics=("arbitrary", "arbitrary"),
                                             vmem_limit_bytes=VMEM_LIMIT_BYTES),
        name="dsa",
    )(qi_t, w_t, qa_t, kiwi, ka, va_t, _block_ones(N_KV_A))


def _stick_kernel(q_ref, k_ref, v_ref, tri_ref, ones_ref, o_ref, acc_ref, kmax_ref,
                  *, pos0, tq, tk, heads):
    i = pl.program_id(2)

    @pl.when(i == 0)
    def _():
        kmax_ref[...] = jnp.broadcast_to(_max_head_norms(k_ref, ones_ref, SUPER * tk), kmax_ref.shape)

    blank = jnp.zeros((HEAD_DIM, tq), BF16)
    q, bounds = [], []
    for h in range(heads):
        qh = q_ref[h * HEAD_DIM:(h + 1) * HEAD_DIM, :]
        parts = [blank] * heads
        parts[h] = qh
        q.append(jnp.concatenate(parts, axis=0))
        qf = qh.astype(F32)
        qn = jnp.sqrt(jnp.sum(qf * qf, axis=0, keepdims=True))
        bounds.append(qn * kmax_ref[0:1, h * HEAD_DIM:h * HEAD_DIM + 1] * STICK_BOUND_SLACK)
    pos_first = pos0 + i * tq
    pos_row = pos_first + lax.broadcasted_iota(I32, (1, tq), 1)
    row_iota = lax.broadcasted_iota(I32, (tk, tq), 0)
    j_first = jnp.maximum(pos_first + tq - 2, 0) // tk
    tri = tri_ref[...]
    acc_ref[...] = jnp.zeros(acc_ref.shape, F32)

    def cond(state):
        j, live, _ = state
        return jnp.logical_and(j >= 0, live > 0)

    def body(state):
        j, _, carries = state
        r0 = pl.multiple_of(j * tk, tk)
        vis = row_iota + r0 < pos_row
        hs = range(heads)
        k = k_ref[pl.ds(r0, tk), :]
        z = [_dot(k, q[h]) for h in hs]
        softplus = [jnp.maximum(z[h], 0.0) + jnp.log2(1.0 + jnp.exp2(-jnp.abs(z[h]))) for h in hs]
        log_keep = [jnp.where(vis, -softplus[h], 0.0) for h in hs]
        incl = [_dot_hilo_left(tri, log_keep[h]) for h in hs]
        a = [jnp.where(vis, jnp.exp2(z[h] + incl[h] + carries[h]), 0.0).astype(BF16) for h in hs]
        pv = [_dot(v_ref[h * HEAD_DIM:(h + 1) * HEAD_DIM, pl.ds(r0, tk)], a[h]) for h in hs]
        for h in hs:
            acc_ref[h] += pv[h]
        new_carries = [carries[h] + incl[h][0:1, :] for h in hs]
        worst = bounds[0] + new_carries[0]
        for h in range(1, heads):
            worst = jnp.maximum(worst, bounds[h] + new_carries[h])
        live = (jnp.max(worst) >= STICK_CUTOFF).astype(I32)
        return j - 1, live, tuple(new_carries)

    init = (j_first, jnp.int32(1), (jnp.zeros((1, tq), F32),) * heads)
    lax.while_loop(cond, body, init)
    for h in range(heads):
        o_ref[h * HEAD_DIM:(h + 1) * HEAD_DIM, :] = acc_ref[h].astype(o_ref.dtype)


def _stick(q_t, k, v_t, *, pos0, tq, tk):
    b, width, t = q_t.shape
    lp = k.shape[1]
    hb = STICK_HEADS
    wb = hb * HEAD_DIM
    assert width % wb == 0
    tri = jnp.asarray(np.triu(np.ones((tk, tk), np.float32)), BF16)
    kern = functools.partial(_stick_kernel, pos0=pos0, tq=tq, tk=tk, heads=hb)
    once = pl.Buffered(1)
    return pl.pallas_call(
        kern,
        grid=(b, width // wb, t // tq),
        in_specs=[pl.BlockSpec((None, wb, tq), lambda bi, h, i: (bi, h, i)),
                  pl.BlockSpec((None, lp, wb), lambda bi, h, i: (bi, 0, h), pipeline_mode=once),
                  pl.BlockSpec((None, wb, lp), lambda bi, h, i: (bi, h, 0), pipeline_mode=once),
                  _const_spec((tk, tk)), _const_spec((wb, wb))],
        out_specs=pl.BlockSpec((None, wb, tq), lambda bi, h, i: (bi, h, i)),
        out_shape=jax.ShapeDtypeStruct((b, width, t), BF16),
        scratch_shapes=[pltpu.VMEM((hb, HEAD_DIM, tq), F32), pltpu.VMEM((SUBLANES, wb), F32)],
        compiler_params=pltpu.CompilerParams(dimension_semantics=("arbitrary",) * 3,
                                             vmem_limit_bytes=VMEM_LIMIT_BYTES),
        name="stick",
    )(q_t, k, v_t, tri, _block_ones(hb))


def _rms(x, g):
    ms = jnp.mean(x * x, axis=-1, keepdims=True)
    return (x * lax.rsqrt(ms + EPS)) * g


def _tail_kernel(x_ref, oa_ref, ob_ref, ga_ref, gb_ref, p_ref, pa_ref, pb_ref, wo_ref,
                 nf_ref, wu_ref, wd_ref, np_ref, wp_ref, wg_ref, y_ref, *, ff_chunk, feature_major):
    def rows(ref):
        return ref[...].astype(F32).T.astype(BF16) if feature_major else ref[...]

    mix = (jax.nn.sigmoid(ga_ref[...]) * _dot(rows(oa_ref), pa_ref[...])
           + jax.nn.sigmoid(gb_ref[...]) * _dot(rows(ob_ref), pb_ref[...]))
    x = x_ref[...] + _dot(mix.astype(BF16), wo_ref[...])
    h = _rms(x, nf_ref[...]).astype(BF16)
    d_ff = wu_ref.shape[1]
    ffn = jnp.zeros_like(x)
    for c in range(d_ff // ff_chunk):
        u = jnp.maximum(_dot(h, wu_ref[:, c * ff_chunk:(c + 1) * ff_chunk]), 0.0)
        ffn = ffn + _dot((u * u).astype(BF16), wd_ref[c * ff_chunk:(c + 1) * ff_chunk, :])
    x = x + ffn
    h = _rms(x, np_ref[...]).astype(BF16)
    y_ref[...] = x + _dot(p_ref[...].astype(BF16), wp_ref[...]) * jax.nn.sigmoid(_dot(h, wg_ref[...]))


def _tail(x, o_a, o_b, ga, gb, p, proj_a, proj_b, w_out, norm_ffn, w_up, w_down,
          norm_ple, w_ple, w_gate):
    b, t, d_model = x.shape
    n = b * t
    tm = min(TOKEN_TILE, n)
    assert n % tm == 0
    feature_major = t % tm == 0
    d_ff = w_up.shape[1]
    row = lambda w: pl.BlockSpec((tm, w), lambda i: (i, 0))
    if feature_major:
        tiles_per_seq = t // tm
        branch = lambda w: pl.BlockSpec((None, w, tm), lambda i: (i // tiles_per_seq, 0, i % tiles_per_seq))
    else:
        o_a, o_b = (o.transpose(0, 2, 1).reshape(n, o.shape[1]) for o in (o_a, o_b))
        branch = row
    weights = (proj_a, proj_b, w_out, norm_ffn.reshape(1, d_model), w_up, w_down,
               norm_ple.reshape(1, d_model), w_ple, w_gate)
    return pl.pallas_call(
        functools.partial(_tail_kernel, ff_chunk=min(1024, d_ff), feature_major=feature_major),
        grid=(n // tm,),
        in_specs=[row(d_model), branch(WIDTH_A), branch(WIDTH_B), row(d_model), row(d_model),
                  row(p.shape[-1])] + [_const_spec(w.shape) for w in weights],
        out_specs=row(d_model),
        out_shape=jax.ShapeDtypeStruct((n, d_model), F32),
        compiler_params=pltpu.CompilerParams(dimension_semantics=("arbitrary",),
                                             vmem_limit_bytes=VMEM_LIMIT_BYTES),
        name="tail",
    )(x.reshape(n, d_model), o_a, o_b, ga, gb, p.reshape(n, p.shape[-1]), *weights).reshape(b, t, d_model)


def _layer(x, p, pos0, caches, wts):
    (norm_mix, w_in_p, g_qa, g_ka, proj_a, proj_b, w_out, norm_ffn, w_up, w_down,
     norm_ple, w_ple, w_gate) = wts
    b, t, d_model = x.shape
    pos = pos0 + jnp.arange(t, dtype=I32)
    pr = _project(x, pos, norm_mix, w_in_p, g_qa, g_ka)

    per_seq = lambda a: a.reshape(b, t, a.shape[-1])
    new_state = (per_seq(pr["ka"]).reshape(b, t, N_KV_A, HEAD_DIM),
                 per_seq(pr["va"]).reshape(b, t, N_KV_A, HEAD_DIM),
                 per_seq(pr["kiwi"])[:, :, :IDX_DIM],
                 per_seq(pr["kb"]).reshape(b, t, N_HEADS_B, HEAD_DIM),
                 per_seq(pr["vb"]).reshape(b, t, N_HEADS_B, HEAD_DIM))
    kiwi, ka, kb = per_seq(pr["kiwi_bf"]), per_seq(pr["ka_bf"]), per_seq(pr["kb_bf"])
    va_t, vb_t = pr["va_bf"], pr["vb_bf"]
    tk = KEY_TILE
    if caches is not None:
        c_ka, c_va, c_ki, c_kb, c_vb = caches
        past = c_ka.shape[1]
        flat = lambda c: c.reshape(b, past, -1).astype(BF16)
        ki_lanes = jnp.pad(flat(c_ki), ((0, 0), (0, 0), (0, LANES - IDX_DIM)))
        kiwi = jnp.concatenate([ki_lanes, kiwi], axis=1)
        ka = jnp.concatenate([flat(c_ka), ka], axis=1)
        kb = jnp.concatenate([flat(c_kb), kb], axis=1)
        va_t = jnp.concatenate([flat(c_va).transpose(0, 2, 1), va_t], axis=2)
        vb_t = jnp.concatenate([flat(c_vb).transpose(0, 2, 1), vb_t], axis=2)
    n_keys = ka.shape[1]
    lp = -(-n_keys // (SUPER * tk)) * (SUPER * tk)
    if lp != n_keys:
        pad_rows = lambda a: jnp.pad(a, ((0, 0), (0, lp - n_keys), (0, 0)))
        pad_cols = lambda a: jnp.pad(a, ((0, 0), (0, 0), (0, lp - n_keys)))
        kiwi, ka, kb, va_t, vb_t = pad_rows(kiwi), pad_rows(ka), pad_rows(kb), pad_cols(va_t), pad_cols(vb_t)
    n_sel = min(TOPK_MAX, n_keys // 4)

    o_a = _dsa(pr["qi"], pr["w"], pr["qa"], kiwi, ka, va_t, pos0=pos0, n_keys=n_keys, n_sel=n_sel,
               tq=min(Q_TILE_A, t), tk=tk)
    o_b = _stick(pr["qb"], kb, vb_t, pos0=pos0, tq=min(Q_TILE_B, t), tk=tk)
    y = _tail(x, o_a, o_b, pr["ga"], pr["gb"], p, proj_a, proj_b, w_out, norm_ffn,
              w_up, w_down, norm_ple, w_ple, w_gate)
    return y, new_state


def kernel(x_prompt, x_sample, cache_a_k, cache_a_v, cache_a_kidx, cache_b_k, cache_b_v, p_prompt, p_sample, norm_mix, w_in, g_qa, g_ka, proj_a, proj_b, w_out, norm_ffn, w_up, w_down, norm_ple, w_ple, w_ple_gate):
    depth = w_in.shape[0]
    past = cache_a_k.shape[2]
    d_model = x_prompt.shape[-1]
    split = _SEG["kiwi"][0] + IDX_DIM + N_IDX_HEADS
    assert w_in.shape[2] == split + 3 * WIDTH_B + 2 * d_model
    xp, xs = x_prompt, x_sample
    outs_p, outs_s = [], []
    for i in range(depth):
        w_in_p = jnp.concatenate(
            [w_in[i, :, :split], jnp.zeros((d_model, LANES - IDX_DIM - N_IDX_HEADS), w_in.dtype),
             w_in[i, :, split:]], axis=1).astype(BF16)
        wts = (norm_mix[i], w_in_p, g_qa[i], g_ka[i], proj_a[i].astype(BF16), proj_b[i].astype(BF16),
               w_out[i].astype(BF16), norm_ffn[i], w_up[i].astype(BF16), w_down[i].astype(BF16),
               norm_ple[i], w_ple[i].astype(BF16), w_ple_gate[i].astype(BF16))
        xp, st = _layer(xp, p_prompt[i], 0, None, wts)
        outs_p.append(st)
        caches = (cache_a_k[i], cache_a_v[i], cache_a_kidx[i], cache_b_k[i], cache_b_v[i])
        xs, st = _layer(xs, p_sample[i], past, caches, wts)
        outs_s.append(st)
    stack = lambda outs, k: jnp.stack([o[k] for o in outs], 0)
    return ((xp, xs) + tuple(stack(outs_p, k) for k in range(5))
            + tuple(stack(outs_s, k) for k in range(5)))
```

```python
import functools

import numpy as np
import jax
import jax.numpy as jnp
from jax import lax
from jax.experimental import pallas as pl
from jax.experimental.pallas import tpu as pltpu

CHUNK = 64
HEAD_DIM = 64
N_HEADS_A = 8
N_KV_A = 2
GROUP_A = N_HEADS_A // N_KV_A
N_HEADS_B = 8
N_IDX_HEADS = 8
IDX_DIM = 32
TOPK_MAX = 256
ROPE_THETA = 500000.0
EPS = 1e-6
WIDTH_A = N_HEADS_A * HEAD_DIM
WIDTH_B = N_HEADS_B * HEAD_DIM
WIDTH_I = N_IDX_HEADS * IDX_DIM
KV_A = N_KV_A * HEAD_DIM

LANES = 128
SUBLANES = 8
BF16_ROWS = 16
VMEM_LIMIT_BYTES = 56 * 1024 * 1024

TOKEN_TILE = 256
KEY_TILE = 256
Q_TILE_A = 256
Q_TILE_B = 256
SUPER = 4
COUNT_ROWS = 64
STICK_HEADS = 4
LOG2E = 1.4426950408889634
SAFE_EXP2 = 60.0
NORM_SLACK = 1.01
FORCE_MASK = 7
SEARCH_CAP = 400
NARROW_BRACKET = 48
SUB_ULP_STEPS = 6

MASKED = -1e30
STICK_CUTOFF = -120.0 * LOG2E
STICK_BOUND_SLACK = 1.01

F32 = jnp.float32
BF16 = jnp.bfloat16
I32 = jnp.int32


def _const_spec(shape):
    nd = len(shape)
    return pl.BlockSpec(shape, lambda *_: (0,) * nd, pipeline_mode=pl.Buffered(1))


def _dot(a, b):
    return jnp.dot(a, b, preferred_element_type=F32)


def _dot_hilo(x, m):
    hi = x.astype(BF16)
    lo = (x - hi.astype(F32)).astype(BF16)
    return _dot(hi, m) + _dot(lo, m)


def _dot_hilo_left(m, x):
    hi = x.astype(BF16)
    lo = (x - hi.astype(F32)).astype(BF16)
    return _dot(m, hi) + _dot(m, lo)


def _block_ones(n_blocks, value=1.0):
    return jnp.asarray(np.kron(np.eye(n_blocks, dtype=np.float32),
                               np.full((HEAD_DIM, HEAD_DIM), value, np.float32)), BF16)


def _max_head_norms(k_ref, ones_ref, chunk):
    ones = ones_ref[...]

    def body(j, mx):
        kf = k_ref[pl.ds(pl.multiple_of(j * chunk, chunk), chunk), :].astype(F32)
        sq = _dot((kf * kf).astype(BF16), ones)
        part = sq[0:SUBLANES]
        for r in range(1, chunk // SUBLANES):
            part = jnp.maximum(part, sq[r * SUBLANES:(r + 1) * SUBLANES])
        return jnp.maximum(mx, part)
    mx = lax.fori_loop(0, k_ref.shape[0] // chunk, body, jnp.zeros((SUBLANES, k_ref.shape[1]), F32))
    return jnp.sqrt(jnp.max(mx, axis=0, keepdims=True) * NORM_SLACK)


_SEG = {}
_off = 0
for _name, _w in (("qa", WIDTH_A), ("ka", KV_A), ("va", KV_A), ("qi", WIDTH_I),
                  ("kiwi", LANES), ("qb", WIDTH_B), ("kb", WIDTH_B), ("vb", WIDTH_B), ("gates", 0)):
    _SEG[_name] = (_off, _w)
    _off += _w
del _off, _name, _w
IDX_WEIGHT_SCALE = (N_IDX_HEADS ** -0.5) * (IDX_DIM ** -0.5)
Q_SCALE = HEAD_DIM ** -0.5 * LOG2E


def _rope(y, c, s1, s2, half):
    cols = []
    for k in range(y.shape[1] // LANES):
        yk = y[:, k * LANES:(k + 1) * LANES]
        cols.append(yk * c + pltpu.roll(yk, LANES - half, 1) * s1 + pltpu.roll(yk, half, 1) * s2)
    return cols[0] if len(cols) == 1 else jnp.concatenate(cols, axis=1)


def _proj_kernel(x_ref, g_ref, w_ref, gq_ref, gk_ref, gm_ref, tab_ref,
                 ka_ref, va_ref, kiwi_ref, kb_ref, vb_ref, ga_ref, gb_ref,
                 kab_ref, kiwib_ref, kbb_ref, qa_ref, qi_ref, qb_ref, w_out_ref, vab_ref, vbb_ref,
                 *, d_model, feature_major):
    x = x_ref[...]
    ms = jnp.mean(x * x, axis=-1, keepdims=True)
    h = ((x * lax.rsqrt(ms + EPS)) * g_ref[...]).astype(BF16)

    def seg(name):
        off, w = _SEG[name]
        return _dot(h, w_ref[:, off:off + w])

    def tab(k):
        return tab_ref[:, k * LANES:(k + 1) * LANES]

    def head_norm(u, gain):
        w = u.shape[1]
        msq = _dot_hilo(u * u, gm_ref[:w, :w])
        return (u * lax.rsqrt(msq + EPS)) * gain

    def put(ref, val):
        ref[...] = (val.T if feature_major else val).astype(ref.dtype)

    qa = head_norm(seg("qa"), gq_ref[...])
    put(qa_ref, _rope(qa, tab(0), tab(1), tab(2), HEAD_DIM // 8) * Q_SCALE)
    ka = _rope(head_norm(seg("ka"), gk_ref[...]), tab(0), tab(1), tab(2), HEAD_DIM // 8)
    ka_ref[...] = ka
    kab_ref[...] = ka.astype(BF16)
    va = seg("va")
    va_ref[...] = va
    put(vab_ref, va)
    put(qi_ref, _rope(seg("qi"), tab(3), tab(4), tab(5), IDX_DIM // 8))
    kiwi = _rope(seg("kiwi"), tab(6), tab(7), tab(8), IDX_DIM // 8)
    kiwi_ref[...] = kiwi
    kiwib_ref[...] = kiwi.astype(BF16)
    wi = kiwi * IDX_WEIGHT_SCALE
    if feature_major:
        w_out_ref[...] = wi.T[IDX_DIM:IDX_DIM + N_IDX_HEADS, :]
    else:
        w_out_ref[...] = wi
    put(qb_ref, seg("qb") * Q_SCALE)
    kb = seg("kb")
    kb_ref[...] = kb
    kbb_ref[...] = kb.astype(BF16)
    vb = seg("vb")
    vb_ref[...] = vb
    put(vbb_ref, vb)
    g_off = _SEG["gates"][0]
    ga_ref[...] = _dot(h, w_ref[:, g_off:g_off + d_model])
    gb_ref[...] = _dot(h, w_ref[:, g_off + d_model:g_off + 2 * d_model])


def _rope_tables(pos):
    def one(d):
        r = d // 4
        hr = r // 2
        inv = jnp.power(jnp.float32(ROPE_THETA), -jnp.arange(hr, dtype=F32) * (2.0 / r))
        ang = pos.astype(F32)[:, None] * inv[None, :]
        cos, sin = jnp.cos(ang), jnp.sin(ang)
        t = pos.shape[0]
        one_ = jnp.ones((t, d - r), F32)
        zero = jnp.zeros((t, d - r), F32)
        zh = jnp.zeros((t, hr), F32)
        c = jnp.concatenate([cos, cos, one_], axis=1)
        s1 = jnp.concatenate([-sin, zh, zero], axis=1)
        s2 = jnp.concatenate([zh, sin, zero], axis=1)
        return c, s1, s2

    c64, s164, s264 = one(HEAD_DIM)
    c32, s132, s232 = one(IDX_DIM)
    t = pos.shape[0]
    rep = lambda a, d: jnp.tile(a, (1, LANES // d))
    padc = jnp.ones((t, LANES - IDX_DIM), F32)
    padz = jnp.zeros((t, LANES - IDX_DIM), F32)
    return jnp.concatenate([
        rep(c64, HEAD_DIM), rep(s164, HEAD_DIM), rep(s264, HEAD_DIM),
        rep(c32, IDX_DIM), rep(s132, IDX_DIM), rep(s232, IDX_DIM),
        jnp.concatenate([c32, padc], axis=1), jnp.concatenate([s132, padz], axis=1),
        jnp.concatenate([s232, padz], axis=1)], axis=1)


def _project(x, pos, norm_mix, w_in_p, g_qa, g_ka):
    b, t, d_model = x.shape
    n = b * t
    tm = min(TOKEN_TILE, n)
    assert n % tm == 0 and (t % tm == 0 or tm % t == 0)
    feature_major = t % tm == 0
    tables = _rope_tables(pos)
    if t < tm:
        tables = jnp.tile(tables, (tm // t, 1))
    tiles_per_seq = max(t // tm, 1)
    gm = _block_ones(N_HEADS_A, 1.0 / HEAD_DIM)
    gq = jnp.tile(g_qa.reshape(1, HEAD_DIM), (1, N_HEADS_A))
    gk = jnp.tile(g_ka.reshape(1, HEAD_DIM), (1, N_KV_A))

    row = lambda w: pl.BlockSpec((tm, w), lambda i: (i, 0))
    rows = lambda w, dt: (jax.ShapeDtypeStruct((n, w), dt), row(w))
    if feature_major:
        cols = lambda w, dt: (jax.ShapeDtypeStruct((b, w, t), dt),
                              pl.BlockSpec((None, w, tm), lambda i: (i // tiles_per_seq, 0, i % tiles_per_seq)))
        w_out = cols(N_IDX_HEADS, F32)
    else:
        cols = rows
        w_out = rows(LANES, F32)
    names = ["ka", "va", "kiwi", "kb", "vb", "ga", "gb", "ka_bf", "kiwi_bf", "kb_bf",
             "qa", "qi", "qb", "w", "va_bf", "vb_bf"]
    outs = [rows(KV_A, F32), rows(KV_A, F32), rows(LANES, F32), rows(WIDTH_B, F32), rows(WIDTH_B, F32),
            rows(d_model, F32), rows(d_model, F32), rows(KV_A, BF16), rows(LANES, BF16), rows(WIDTH_B, BF16),
            cols(WIDTH_A, BF16), cols(WIDTH_I, BF16), cols(WIDTH_B, BF16), w_out,
            cols(KV_A, BF16), cols(WIDTH_B, BF16)]
    res = pl.pallas_call(
        functools.partial(_proj_kernel, d_model=d_model, feature_major=feature_major),
        grid=(n // tm,),
        in_specs=[row(d_model), _const_spec((1, d_model)), _const_spec(w_in_p.shape),
                  _const_spec((1, WIDTH_A)), _const_spec((1, KV_A)), _const_spec((WIDTH_A, WIDTH_A)),
                  pl.BlockSpec((tm, 9 * LANES), lambda i: (i % tiles_per_seq, 0))],
        out_specs=[o[1] for o in outs],
        out_shape=[o[0] for o in outs],
        compiler_params=pltpu.CompilerParams(dimension_semantics=("arbitrary",),
                                             vmem_limit_bytes=VMEM_LIMIT_BYTES),
        name="proj",
    )(x.reshape(n, d_model), norm_mix.reshape(1, d_model), w_in_p, gq, gk, gm, tables)
    p = dict(zip(names, res))
    if not feature_major:
        to_cols = lambda a: a.reshape(b, t, a.shape[-1]).transpose(0, 2, 1)
        for k in ("qa", "qi", "qb", "va_bf", "vb_bf"):
            p[k] = to_cols(p[k])
        p["w"] = to_cols(p["w"][:, IDX_DIM:IDX_DIM + N_IDX_HEADS])
    return p


def _dsa_kernel(qi_ref, w_ref, qa_ref, ki_ref, ka_ref, va_ref, ones_ref, o_ref,
                sc_ref, acc_ref, kmax_ref, *, pos0, n_keys, n_sel, tq, tk, idx_bits):
    i = pl.program_id(1)
    pos_first = pos0 + i * tq
    lim_max = jnp.minimum(((pos_first + tq - 1) // CHUNK + 1) * CHUNK, n_keys)
    n_tiles = (lim_max + tk - 1) // tk
    lane_pos = pos_first + lax.broadcasted_iota(I32, (1, tq), 1)
    lim_row = jnp.minimum((lax.shift_right_arithmetic(lane_pos, int(np.log2(CHUNK))) + 1) * CHUNK, n_keys)
    neg_inf = jnp.float32(-jnp.inf)

    n_super = (n_tiles + SUPER - 1) // SUPER
    row_iota = lax.broadcasted_iota(I32, (tk, tq), 0)

    def lanes_by_head(ref, first, n_heads, width):
        return jnp.concatenate([ref[(first + h) * width:(first + h + 1) * width, :] for h in range(n_heads)],
                               axis=1)

    qi = lanes_by_head(qi_ref, 0, N_IDX_HEADS, IDX_DIM)
    qi = jnp.concatenate([qi, jnp.zeros((LANES - IDX_DIM, N_IDX_HEADS * tq), BF16)], axis=0)
    blank = jnp.zeros((HEAD_DIM, GROUP_A * tq), BF16)
    qa = []
    for g in range(N_KV_A):
        parts = [blank] * N_KV_A
        parts[g] = lanes_by_head(qa_ref, g * GROUP_A, GROUP_A, HEAD_DIM)
        qa.append(jnp.concatenate(parts, axis=0))
    w = w_ref[...]

    def fold_rows(x, op):
        out = x[0:SUBLANES]
        for r in range(1, x.shape[0] // SUBLANES):
            out = op(out, x[r * SUBLANES:(r + 1) * SUBLANES])
        return out

    def score_tiles(js, carry):
        smax, smin = carry
        for c in range(SUPER):
            r0 = pl.multiple_of((js * SUPER + c) * tk, tk)
            lg = _dot(ki_ref[pl.ds(r0, tk), :], qi)
            sc = jnp.zeros((tk, tq), F32)
            for h in range(N_IDX_HEADS):
                sc = sc + jnp.maximum(lg[:, h * tq:(h + 1) * tq], 0.0) * w[h:h + 1, :]
            adm = row_iota + r0 < lim_row
            masked = jnp.where(adm, sc, neg_inf)
            sc_ref[pl.ds(r0, tk), :] = masked
            smax = jnp.maximum(smax, fold_rows(masked, jnp.maximum))
            smin = jnp.minimum(smin, fold_rows(jnp.where(adm, sc, -neg_inf), jnp.minimum))
        return smax, smin

    smax, smin = lax.fori_loop(0, n_super, score_tiles,
                               (jnp.full((SUBLANES, tq), neg_inf, F32), jnp.full((SUBLANES, tq), -neg_inf, F32)))
    smax = jnp.max(smax, axis=0, keepdims=True)
    smin = jnp.min(smin, axis=0, keepdims=True)

    n_steps = (n_tiles + 1) // 2
    tk2 = 2 * tk
    row_iota2 = lax.broadcasted_iota(I32, (tk2, tq), 0)

    def count(*preds):
        def body(j, accs):
            r0 = pl.multiple_of(j * tk2, tk2)
            s = sc_ref[pl.ds(r0, tk2), :]
            out = []
            for pred, acc in zip(preds, accs):
                hit = jnp.where(pred(s, r0), 1, 0).astype(I32)
                for r in range(tk2 // COUNT_ROWS):
                    acc = acc + hit[r * COUNT_ROWS:(r + 1) * COUNT_ROWS]
                out.append(acc)
            return tuple(out)
        accs = lax.fori_loop(0, n_steps, body, (jnp.zeros((COUNT_ROWS, tq), I32),) * len(preds))
        return [jnp.sum(a, axis=0, keepdims=True) for a in accs]

    def count_and_max_below(cand, hi):
        def body(j, carry):
            acc, best = carry
            s = sc_ref[pl.ds(pl.multiple_of(j * tk2, tk2), tk2), :]
            hit = jnp.where(s >= cand, 1, 0).astype(I32)
            low = jnp.where(s < hi, s, neg_inf)
            for r in range(tk2 // COUNT_ROWS):
                acc = acc + hit[r * COUNT_ROWS:(r + 1) * COUNT_ROWS]
                best = jnp.maximum(best, low[r * COUNT_ROWS:(r + 1) * COUNT_ROWS])
            return acc, best
        acc, best = lax.fori_loop(0, n_steps, body, (jnp.zeros((COUNT_ROWS, tq), I32),
                                                     jnp.full((COUNT_ROWS, tq), neg_inf, F32)))
        return jnp.sum(acc, axis=0, keepdims=True), jnp.max(best, axis=0, keepdims=True)

    def to_key(x):
        bits = pltpu.bitcast(x, I32)
        return bits ^ (lax.shift_right_arithmetic(bits, 31) & 0x7FFFFFFF)

    def from_key(k):
        return pltpu.bitcast(k ^ (lax.shift_right_arithmetic(k, 31) & 0x7FFFFFFF), F32)

    k_f = jnp.float32(n_sel)
    n_adm = lim_row
    zero = jnp.zeros((1, tq), F32)
    fge0, fgt0 = count(lambda s, r0: s >= zero, lambda s, r0: s > zero)
    take_all = n_adm <= n_sel
    zero_cut = fgt0 == n_sel
    zero_tie = jnp.logical_and(fgt0 < n_sel, fge0 >= n_sel)
    pos_side = fgt0 > n_sel
    neg_side = fge0 < n_sel
    as_i = lambda m: jnp.where(m, 1, 0).astype(I32)
    state = dict(
        it=jnp.int32(0),
        lo=jnp.where(pos_side, 0.0, smin), flo=jnp.where(pos_side, fge0, n_adm),
        hi=jnp.where(neg_side, 0.0, from_key(to_key(smax) + 1)), fhi=jnp.where(neg_side, fge0, 0),
        wlo=jnp.ones((1, tq), F32), whi=jnp.ones((1, tq), F32), last=jnp.zeros((1, tq), I32),
        stuck=jnp.zeros((1, tq), I32),
        probe=jnp.minimum(lane_pos, 0), probed=jnp.minimum(lane_pos, 0), below=smax * 0.0,
        done=as_i(jnp.logical_or(take_all, jnp.logical_or(zero_cut, zero_tie))),
        thr=jnp.where(take_all, neg_inf, 0.0),
        tie=as_i(jnp.logical_and(zero_tie, jnp.logical_not(take_all))),
        need=n_sel - fgt0)

    def searching(st):
        return jnp.logical_and(jnp.min(st["done"]) == 0, st["it"] < SEARCH_CAP)

    def search_step(st):
        lo, hi, flo, fhi = st["lo"], st["hi"], st["flo"], st["fhi"]
        flo_f, fhi_f = flo.astype(F32), fhi.astype(F32)
        narrow = flo - fhi <= NARROW_BRACKET
        glo = jnp.where(narrow, flo_f - k_f, jnp.log(flo_f + 0.5) - jnp.log(k_f)) * st["wlo"]
        ghi = jnp.where(narrow, k_f - fhi_f, jnp.log(k_f) - jnp.log(fhi_f + 0.5)) * st["whi"]
        frac = jnp.clip(glo / (glo + ghi), 0.005, 0.995)
        cand = lo + (hi - lo) * frac
        klo, khi, kc = to_key(lo), to_key(hi), to_key(cand)
        adjacent = klo + 1 >= khi
        mid = (lax.shift_right_arithmetic(klo, 1) + lax.shift_right_arithmetic(khi, 1) + (klo & khi & 1))
        outside = jnp.logical_or(jnp.logical_or(kc <= klo, kc >= khi), (st["it"] & FORCE_MASK) == FORCE_MASK)
        cand = jnp.where(jnp.logical_and(outside, jnp.logical_not(adjacent)), from_key(mid), cand)
        open_ = st["done"] == 0
        probing = st["probe"] > 0
        extract = jnp.logical_and(jnp.logical_and(open_, n_sel - fhi == 1),
                                  jnp.logical_and(jnp.logical_not(probing), st["probed"] == 0))
        cand = jnp.where(probing, st["below"], jnp.where(extract, hi, cand))
        cnt, below = lax.cond(jnp.max(as_i(extract)) > 0,
                              lambda: count_and_max_below(cand, hi),
                              lambda: (count(lambda s, r0: s >= cand)[0], jnp.zeros((1, tq), F32)))
        active = jnp.logical_and(open_, jnp.logical_not(extract))
        up = jnp.logical_and(active, cnt > n_sel)
        dn = jnp.logical_and(active, cnt < n_sel)
        hit = jnp.logical_and(active, cnt == n_sel)
        stuck = jnp.where(adjacent, st["stuck"] + 1, 0)
        tied = jnp.logical_and(jnp.logical_and(active, jnp.logical_not(hit)), stuck > SUB_ULP_STEPS)
        last = st["last"]
        new_lo, new_fhi = jnp.where(up, cand, lo), jnp.where(dn, cnt, fhi)
        return dict(
            it=st["it"] + 1,
            lo=new_lo, flo=jnp.where(up, cnt, flo),
            hi=jnp.where(dn, cand, hi), fhi=new_fhi,
            wlo=jnp.where(dn, jnp.where(last == -1, st["wlo"] * 0.5, 1.0), jnp.where(up, 1.0, st["wlo"])),
            whi=jnp.where(up, jnp.where(last == 1, st["whi"] * 0.5, 1.0), jnp.where(dn, 1.0, st["whi"])),
            last=jnp.where(up, 1, jnp.where(dn, -1, last)),
            stuck=stuck,
            probe=as_i(extract), probed=jnp.where(probing, 1, st["probed"]),
            below=jnp.where(extract, below, st["below"]),
            done=jnp.where(jnp.logical_or(hit, tied), 1, st["done"]),
            thr=jnp.where(hit, cand, jnp.where(tied, new_lo, st["thr"])),
            tie=jnp.where(tied, 1, st["tie"]),
            need=jnp.where(tied, n_sel - new_fhi, st["need"]))

    state = lax.while_loop(searching, search_step, state)
    thr, tie, need = state["thr"], state["tie"], state["need"]

    def tie_cut():
        def step(t, c0):
            cand = c0 | lax.shift_left(jnp.int32(1), idx_bits - 1 - t)
            (cnt,) = count(lambda s, r0: jnp.where(s == thr, row_iota2 + r0, cand) < cand)
            return jnp.where(cnt < need, cand, c0)
        return lax.fori_loop(0, idx_bits, step, jnp.zeros((1, tq), I32))

    any_tie = jnp.max(tie) > 0
    cut = lax.cond(any_tie, tie_cut, lambda: jnp.zeros((1, tq), I32))
    cut = jnp.where(jnp.logical_or(take_all, zero_cut), -1, jnp.where(tie > 0, cut, jnp.int32(2 ** 30)))

    acc_ref[...] = jnp.zeros(acc_ref.shape, F32)
    ones_rows = jnp.ones((BF16_ROWS, tk), BF16)

    def masked_scores(j):
        r0 = pl.multiple_of(j * tk, tk)
        sc = sc_ref[pl.ds(r0, tk), :]
        bias = jnp.where(sc > thr, 0.0,
                         jnp.where(sc == thr, jnp.where(row_iota + r0 <= cut, 0.0, MASKED), MASKED))
        bias = jnp.concatenate([bias] * GROUP_A, axis=1)
        k = ka_ref[pl.ds(r0, tk), :]
        return [_dot(k, qa[g]) + bias for g in range(N_KV_A)]

    def values(g, j):
        v = va_ref[g * HEAD_DIM:(g + 1) * HEAD_DIM, pl.ds(pl.multiple_of(j * tk, tk), tk)]
        return jnp.concatenate([v, ones_rows], axis=0)

    def attend_bounded():
        def tiles(js, _):
            js4 = [js * SUPER + c for c in range(SUPER)]
            scores = [masked_scores(j) for j in js4]
            probs = [[jnp.exp2(s).astype(BF16) for s in per_group] for per_group in scores]
            for g in range(N_KV_A):
                pv = _dot(values(g, js4[0]), probs[0][g])
                for c in range(1, SUPER):
                    pv = pv + _dot(values(g, js4[c]), probs[c][g])
                acc_ref[g] += pv
            return 0
        lax.fori_loop(0, n_super, tiles, 0)

    def attend_online():
        def tile(j, ms):
            new_ms = []
            for g, s in enumerate(masked_scores(j)):
                m_new = jnp.maximum(ms[g], jnp.max(s, axis=0, keepdims=True))
                p = jnp.exp2(s - m_new).astype(BF16)
                acc_ref[g] = acc_ref[g] * jnp.exp2(ms[g] - m_new) + _dot(values(g, j), p)
                new_ms.append(m_new)
            return tuple(new_ms)
        m0 = jnp.full((1, GROUP_A * tq), MASKED, F32)
        lax.fori_loop(0, n_tiles, tile, (m0,) * N_KV_A)

    @pl.when(i == 0)
    def _():
        kmax_ref[...] = jnp.broadcast_to(_max_head_norms(ka_ref, ones_ref, SUPER * tk), kmax_ref.shape)

    score_bound = jnp.float32(0.0)
    for g in range(N_KV_A):
        qf = qa[g].astype(F32)
        qn = jnp.sqrt(jnp.sum(qf * qf, axis=0, keepdims=True))
        score_bound = jnp.maximum(score_bound, jnp.max(qn * kmax_ref[0:1, g * HEAD_DIM:g * HEAD_DIM + 1]))
    lax.cond(score_bound <= SAFE_EXP2, attend_bounded, attend_online)
    for g in range(N_KV_A):
        out = acc_ref[g, :HEAD_DIM] / acc_ref[g, HEAD_DIM:HEAD_DIM + 1]
        for hh in range(GROUP_A):
            r = (g * GROUP_A + hh) * HEAD_DIM
            o_ref[r:r + HEAD_DIM, :] = out[:, hh * tq:(hh + 1) * tq].astype(o_ref.dtype)


def _dsa(qi_t, w_t, qa_t, kiwi, ka, va_t, *, pos0, n_keys, n_sel, tq, tk):
    b, _, t = qa_t.shape
    lp = ka.shape[1]
    kern = functools.partial(_dsa_kernel, pos0=pos0, n_keys=n_keys, n_sel=n_sel, tq=tq, tk=tk,
                             idx_bits=int(np.ceil(np.log2(lp))))
    once = pl.Buffered(1)
    per_q = lambda rows: pl.BlockSpec((None, rows, tq), lambda bi, i: (bi, 0, i))
    per_b = lambda s: pl.BlockSpec((None,) + s, lambda bi, i: (bi, 0, 0), pipeline_mode=once)
    return pl.pallas_call(
        kern,
        grid=(b, t // tq),
        in_specs=[per_q(WIDTH_I), per_q(N_IDX_HEADS), per_q(WIDTH_A),
                  per_b((lp, LANES)), per_b((lp, KV_A)), per_b((KV_A, lp)), _const_spec((KV_A, KV_A))],
        out_specs=per_q(WIDTH_A),
        out_shape=jax.ShapeDtypeStruct((b, WIDTH_A, t), BF16),
        scratch_shapes=[pltpu.VMEM((lp, tq), F32),
                        pltpu.VMEM((N_KV_A, HEAD_DIM + BF16_ROWS, GROUP_A * tq), F32),
                        pltpu.VMEM((SUBLANES, KV_A), F32)],
        compiler_params=pltpu.CompilerParams(dimension_semantics=("arbitrary", "arbitrary"),
                                             vmem_limit_bytes=VMEM_LIMIT_BYTES),
        name="dsa",
    )(qi_t, w_t, qa_t, kiwi, ka, va_t, _block_ones(N_KV_A))


def _stick_kernel(q_ref, k_ref, v_ref, tri_ref, ones_ref, o_ref, acc_ref, kmax_ref,
                  *, pos0, tq, tk, heads):
    i = pl.program_id(2)

    @pl.when(i == 0)
    def _():
        kmax_ref[...] = jnp.broadcast_to(_max_head_norms(k_ref, ones_ref, SUPER * tk), kmax_ref.shape)

    blank = jnp.zeros((HEAD_DIM, tq), BF16)
    q, bounds = [], []
    for h in range(heads):
        qh = q_ref[h * HEAD_DIM:(h + 1) * HEAD_DIM, :]
        parts = [blank] * heads
        parts[h] = qh
        q.append(jnp.concatenate(parts, axis=0))
        qf = qh.astype(F32)
        qn = jnp.sqrt(jnp.sum(qf * qf, axis=0, keepdims=True))
        bounds.append(qn * kmax_ref[0:1, h * HEAD_DIM:h * HEAD_DIM + 1] * STICK_BOUND_SLACK)
    pos_first = pos0 + i * tq
    pos_row = pos_first + lax.broadcasted_iota(I32, (1, tq), 1)
    row_iota = lax.broadcasted_iota(I32, (tk, tq), 0)
    j_first = jnp.maximum(pos_first + tq - 2, 0) // tk
    tri = tri_ref[...]
    acc_ref[...] = jnp.zeros(acc_ref.shape, F32)

    def cond(state):
        j, live, _ = state
        return jnp.logical_and(j >= 0, live > 0)

    def body(state):
        j, _, carries = state
        r0 = pl.multiple_of(j * tk, tk)
        vis = row_iota + r0 < pos_row
        hs = range(heads)
        k = k_ref[pl.ds(r0, tk), :]
        z = [_dot(k, q[h]) for h in hs]
        softplus = [jnp.maximum(z[h], 0.0) + jnp.log2(1.0 + jnp.exp2(-jnp.abs(z[h]))) for h in hs]
        log_keep = [jnp.where(vis, -softplus[h], 0.0) for h in hs]
        incl = [_dot_hilo_left(tri, log_keep[h]) for h in hs]
        a = [jnp.where(vis, jnp.exp2(z[h] + incl[h] + carries[h]), 0.0).astype(BF16) for h in hs]
        pv = [_dot(v_ref[h * HEAD_DIM:(h + 1) * HEAD_DIM, pl.ds(r0, tk)], a[h]) for h in hs]
        for h in hs:
            acc_ref[h] += pv[h]
        new_carries = [carries[h] + incl[h][0:1, :] for h in hs]
        worst = bounds[0] + new_carries[0]
        for h in range(1, heads):
            worst = jnp.maximum(worst, bounds[h] + new_carries[h])
        live = (jnp.max(worst) >= STICK_CUTOFF).astype(I32)
        return j - 1, live, tuple(new_carries)

    init = (j_first, jnp.int32(1), (jnp.zeros((1, tq), F32),) * heads)
    lax.while_loop(cond, body, init)
    for h in range(heads):
        o_ref[h * HEAD_DIM:(h + 1) * HEAD_DIM, :] = acc_ref[h].astype(o_ref.dtype)


def _stick(q_t, k, v_t, *, pos0, tq, tk):
    b, width, t = q_t.shape
    lp = k.shape[1]
    hb = STICK_HEADS
    wb = hb * HEAD_DIM
    assert width % wb == 0
    tri = jnp.asarray(np.triu(np.ones((tk, tk), np.float32)), BF16)
    kern = functools.partial(_stick_kernel, pos0=pos0, tq=tq, tk=tk, heads=hb)
    once = pl.Buffered(1)
    return pl.pallas_call(
        kern,
        grid=(b, width // wb, t // tq),
        in_specs=[pl.BlockSpec((None, wb, tq), lambda bi, h, i: (bi, h, i)),
                  pl.BlockSpec((None, lp, wb), lambda bi, h, i: (bi, 0, h), pipeline_mode=once),
                  pl.BlockSpec((None, wb, lp), lambda bi, h, i: (bi, h, 0), pipeline_mode=once),
                  _const_spec((tk, tk)), _const_spec((wb, wb))],
        out_specs=pl.BlockSpec((None, wb, tq), lambda bi, h, i: (bi, h, i)),
        out_shape=jax.ShapeDtypeStruct((b, width, t), BF16),
        scratch_shapes=[pltpu.VMEM((hb, HEAD_DIM, tq), F32), pltpu.VMEM((SUBLANES, wb), F32)],
        compiler_params=pltpu.CompilerParams(dimension_semantics=("arbitrary",) * 3,
                                             vmem_limit_bytes=VMEM_LIMIT_BYTES),
        name="stick",
    )(q_t, k, v_t, tri, _block_ones(hb))


def _rms(x, g):
    ms = jnp.mean(x * x, axis=-1, keepdims=True)
    return (x * lax.rsqrt(ms + EPS)) * g


def _tail_kernel(x_ref, oa_ref, ob_ref, ga_ref, gb_ref, p_ref, pa_ref, pb_ref, wo_ref,
                 nf_ref, wu_ref, wd_ref, np_ref, wp_ref, wg_ref, y_ref, *, ff_chunk, feature_major):
    def rows(ref):
        return ref[...].astype(F32).T.astype(BF16) if feature_major else ref[...]

    mix = (jax.nn.sigmoid(ga_ref[...]) * _dot(rows(oa_ref), pa_ref[...])
           + jax.nn.sigmoid(gb_ref[...]) * _dot(rows(ob_ref), pb_ref[...]))
    x = x_ref[...] + _dot(mix.astype(BF16), wo_ref[...])
    h = _rms(x, nf_ref[...]).astype(BF16)
    d_ff = wu_ref.shape[1]
    ffn = jnp.zeros_like(x)
    for c in range(d_ff // ff_chunk):
        u = jnp.maximum(_dot(h, wu_ref[:, c * ff_chunk:(c + 1) * ff_chunk]), 0.0)
        ffn = ffn + _dot((u * u).astype(BF16), wd_ref[c * ff_chunk:(c + 1) * ff_chunk, :])
    x = x + ffn
    h = _rms(x, np_ref[...]).astype(BF16)
    y_ref[...] = x + _dot(p_ref[...].astype(BF16), wp_ref[...]) * jax.nn.sigmoid(_dot(h, wg_ref[...]))


def _tail(x, o_a, o_b, ga, gb, p, proj_a, proj_b, w_out, norm_ffn, w_up, w_down,
          norm_ple, w_ple, w_gate):
    b, t, d_model = x.shape
    n = b * t
    tm = min(TOKEN_TILE, n)
    assert n % tm == 0
    feature_major = t % tm == 0
    d_ff = w_up.shape[1]
    row = lambda w: pl.BlockSpec((tm, w), lambda i: (i, 0))
    if feature_major:
        tiles_per_seq = t // tm
        branch = lambda w: pl.BlockSpec((None, w, tm), lambda i: (i // tiles_per_seq, 0, i % tiles_per_seq))
    else:
        o_a, o_b = (o.transpose(0, 2, 1).reshape(n, o.shape[1]) for o in (o_a, o_b))
        branch = row
    weights = (proj_a, proj_b, w_out, norm_ffn.reshape(1, d_model), w_up, w_down,
               norm_ple.reshape(1, d_model), w_ple, w_gate)
    return pl.pallas_call(
        functools.partial(_tail_kernel, ff_chunk=min(1024, d_ff), feature_major=feature_major),
        grid=(n // tm,),
        in_specs=[row(d_model), branch(WIDTH_A), branch(WIDTH_B), row(d_model), row(d_model),
                  row(p.shape[-1])] + [_const_spec(w.shape) for w in weights],
        out_specs=row(d_model),
        out_shape=jax.ShapeDtypeStruct((n, d_model), F32),
        compiler_params=pltpu.CompilerParams(dimension_semantics=("arbitrary",),
                                             vmem_limit_bytes=VMEM_LIMIT_BYTES),
        name="tail",
    )(x.reshape(n, d_model), o_a, o_b, ga, gb, p.reshape(n, p.shape[-1]), *weights).reshape(b, t, d_model)


def _layer(x, p, pos0, caches, wts):
    (norm_mix, w_in_p, g_qa, g_ka, proj_a, proj_b, w_out, norm_ffn, w_up, w_down,
     norm_ple, w_ple, w_gate) = wts
    b, t, d_model = x.shape
    pos = pos0 + jnp.arange(t, dtype=I32)
    pr = _project(x, pos, norm_mix, w_in_p, g_qa, g_ka)

    per_seq = lambda a: a.reshape(b, t, a.shape[-1])
    new_state = (per_seq(pr["ka"]).reshape(b, t, N_KV_A, HEAD_DIM),
                 per_seq(pr["va"]).reshape(b, t, N_KV_A, HEAD_DIM),
                 per_seq(pr["kiwi"])[:, :, :IDX_DIM],
                 per_seq(pr["kb"]).reshape(b, t, N_HEADS_B, HEAD_DIM),
                 per_seq(pr["vb"]).reshape(b, t, N_HEADS_B, HEAD_DIM))
    kiwi, ka, kb = per_seq(pr["kiwi_bf"]), per_seq(pr["ka_bf"]), per_seq(pr["kb_bf"])
    va_t, vb_t = pr["va_bf"], pr["vb_bf"]
    tk = KEY_TILE
    if caches is not None:
        c_ka, c_va, c_ki, c_kb, c_vb = caches
        past = c_ka.shape[1]
        flat = lambda c: c.reshape(b, past, -1).astype(BF16)
        ki_lanes = jnp.pad(flat(c_ki), ((0, 0), (0, 0), (0, LANES - IDX_DIM)))
        kiwi = jnp.concatenate([ki_lanes, kiwi], axis=1)
        ka = jnp.concatenate([flat(c_ka), ka], axis=1)
        kb = jnp.concatenate([flat(c_kb), kb], axis=1)
        va_t = jnp.concatenate([flat(c_va).transpose(0, 2, 1), va_t], axis=2)
        vb_t = jnp.concatenate([flat(c_vb).transpose(0, 2, 1), vb_t], axis=2)
    n_keys = ka.shape[1]
    lp = -(-n_keys // (SUPER * tk)) * (SUPER * tk)
    if lp != n_keys:
        pad_rows = lambda a: jnp.pad(a, ((0, 0), (0, lp - n_keys), (0, 0)))
        pad_cols = lambda a: jnp.pad(a, ((0, 0), (0, 0), (0, lp - n_keys)))
        kiwi, ka, kb, va_t, vb_t = pad_rows(kiwi), pad_rows(ka), pad_rows(kb), pad_cols(va_t), pad_cols(vb_t)
    n_sel = min(TOPK_MAX, n_keys // 4)

    o_a = _dsa(pr["qi"], pr["w"], pr["qa"], kiwi, ka, va_t, pos0=pos0, n_keys=n_keys, n_sel=n_sel,
               tq=min(Q_TILE_A, t), tk=tk)
    o_b = _stick(pr["qb"], kb, vb_t, pos0=pos0, tq=min(Q_TILE_B, t), tk=tk)
    y = _tail(x, o_a, o_b, pr["ga"], pr["gb"], p, proj_a, proj_b, w_out, norm_ffn,
              w_up, w_down, norm_ple, w_ple, w_gate)
    return y, new_state


def kernel(x_prompt, x_sample, cache_a_k, cache_a_v, cache_a_kidx, cache_b_k, cache_b_v, p_prompt, p_sample, norm_mix, w_in, g_qa, g_ka, proj_a, proj_b, w_out, norm_ffn, w_up, w_down, norm_ple, w_ple, w_ple_gate):
    depth = w_in.shape[0]
    past = cache_a_k.shape[2]
    d_model = x_prompt.shape[-1]
    split = _SEG["kiwi"][0] + IDX_DIM + N_IDX_HEADS
    assert w_in.shape[2] == split + 3 * WIDTH_B + 2 * d_model
    xp, xs = x_prompt, x_sample
    outs_p, outs_s = [], []
    for i in range(depth):
        w_in_p = jnp.concatenate(
            [w_in[i, :, :split], jnp.zeros((d_model, LANES - IDX_DIM - N_IDX_HEADS), w_in.dtype),
             w_in[i, :, split:]], axis=1).astype(BF16)
        wts = (norm_mix[i], w_in_p, g_qa[i], g_ka[i], proj_a[i].astype(BF16), proj_b[i].astype(BF16),
               w_out[i].astype(BF16), norm_ffn[i], w_up[i].astype(BF16), w_down[i].astype(BF16),
               norm_ple[i], w_ple[i].astype(BF16), w_ple_gate[i].astype(BF16))
        xp, st = _layer(xp, p_prompt[i], 0, None, wts)
        outs_p.append(st)
        caches = (cache_a_k[i], cache_a_v[i], cache_a_kidx[i], cache_b_k[i], cache_b_v[i])
        xs, st = _layer(xs, p_sample[i], past, caches, wts)
        outs_s.append(st)
    stack = lambda outs, k: jnp.stack([o[k] for o in outs], 0)
    return ((xp, xs) + tuple(stack(outs_p, k) for k in range(5))
            + tuple(stack(outs_s, k) for k in range(5)))
```

```python
import functools

import numpy as np
import jax
import jax.numpy as jnp
from jax import lax
from jax.experimental import pallas as pl
from jax.experimental.pallas import tpu as pltpu

CHUNK = 64
HEAD_DIM = 64
N_HEADS_A = 8
N_KV_A = 2
GROUP_A = N_HEADS_A // N_KV_A
N_HEADS_B = 8
N_IDX_HEADS = 8
IDX_DIM = 32
TOPK_MAX = 256
ROPE_THETA = 500000.0
EPS = 1e-6
WIDTH_A = N_HEADS_A * HEAD_DIM
WIDTH_B = N_HEADS_B * HEAD_DIM
WIDTH_I = N_IDX_HEADS * IDX_DIM
KV_A = N_KV_A * HEAD_DIM

LANES = 128
SUBLANES = 8
BF16_ROWS = 16
VMEM_LIMIT_BYTES = 56 * 1024 * 1024

TOKEN_TILE = 256
KEY_TILE = 256
Q_TILE_A = 256
Q_TILE_B = 256
SUPER = 4
COUNT_ROWS = 64
STICK_HEADS = 4
LOG2E = 1.4426950408889634
SAFE_EXP2 = 60.0
NORM_SLACK = 1.01
FORCE_MASK = 7
SEARCH_CAP = 400
NARROW_BRACKET = 48
SUB_ULP_STEPS = 6

MASKED = -1e30
STICK_CUTOFF = -120.0 * LOG2E
STICK_BOUND_SLACK = 1.01

F32 = jnp.float32
BF16 = jnp.bfloat16
I32 = jnp.int32


def _const_spec(shape):
    nd = len(shape)
    return pl.BlockSpec(shape, lambda *_: (0,) * nd, pipeline_mode=pl.Buffered(1))


def _dot(a, b):
    return jnp.dot(a, b, preferred_element_type=F32)


def _dot_hilo(x, m):
    hi = x.astype(BF16)
    lo = (x - hi.astype(F32)).astype(BF16)
    return _dot(hi, m) + _dot(lo, m)


def _dot_hilo_left(m, x):
    hi = x.astype(BF16)
    lo = (x - hi.astype(F32)).astype(BF16)
    return _dot(m, hi) + _dot(m, lo)


def _block_ones(n_blocks, value=1.0):
    return jnp.asarray(np.kron(np.eye(n_blocks, dtype=np.float32),
                               np.full((HEAD_DIM, HEAD_DIM), value, np.float32)), BF16)


def _max_head_norms(k_ref, ones_ref, chunk):
    ones = ones_ref[...]

    def body(j, mx):
        kf = k_ref[pl.ds(pl.multiple_of(j * chunk, chunk), chunk), :].astype(F32)
        sq = _dot((kf * kf).astype(BF16), ones)
        part = sq[0:SUBLANES]
        for r in range(1, chunk // SUBLANES):
            part = jnp.maximum(part, sq[r * SUBLANES:(r + 1) * SUBLANES])
        return jnp.maximum(mx, part)
    mx = lax.fori_loop(0, k_ref.shape[0] // chunk, body, jnp.zeros((SUBLANES, k_ref.shape[1]), F32))
    return jnp.sqrt(jnp.max(mx, axis=0, keepdims=True) * NORM_SLACK)


_SEG = {}
_off = 0
for _name, _w in (("qa", WIDTH_A), ("ka", KV_A), ("va", KV_A), ("qi", WIDTH_I),
                  ("kiwi", LANES), ("qb", WIDTH_B), ("kb", WIDTH_B), ("vb", WIDTH_B), ("gates", 0)):
    _SEG[_name] = (_off, _w)
    _off += _w
del _off, _name, _w
IDX_WEIGHT_SCALE = (N_IDX_HEADS ** -0.5) * (IDX_DIM ** -0.5)
Q_SCALE = HEAD_DIM ** -0.5 * LOG2E


def _rope(y, c, s1, s2, half):
    cols = []
    for k in range(y.shape[1] // LANES):
        yk = y[:, k * LANES:(k + 1) * LANES]
        cols.append(yk * c + pltpu.roll(yk, LANES - half, 1) * s1 + pltpu.roll(yk, half, 1) * s2)
    return cols[0] if len(cols) == 1 else jnp.concatenate(cols, axis=1)


def _proj_kernel(x_ref, g_ref, w_ref, gq_ref, gk_ref, gm_ref, tab_ref,
                 ka_ref, va_ref, kiwi_ref, kb_ref, vb_ref, ga_ref, gb_ref,
                 kab_ref, kiwib_ref, kbb_ref, qa_ref, qi_ref, qb_ref, w_out_ref, vab_ref, vbb_ref,
                 *, d_model, feature_major):
    x = x_ref[...]
    ms = jnp.mean(x * x, axis=-1, keepdims=True)
    h = ((x * lax.rsqrt(ms + EPS)) * g_ref[...]).astype(BF16)

    def seg(name):
        off, w = _SEG[name]
        return _dot(h, w_ref[:, off:off + w])

    def tab(k):
        return tab_ref[:, k * LANES:(k + 1) * LANES]

    def head_norm(u, gain):
        w = u.shape[1]
        msq = _dot_hilo(u * u, gm_ref[:w, :w])
        return (u * lax.rsqrt(msq + EPS)) * gain

    def put(ref, val):
        ref[...] = (val.T if feature_major else val).astype(ref.dtype)

    qa = head_norm(seg("qa"), gq_ref[...])
    put(qa_ref, _rope(qa, tab(0), tab(1), tab(2), HEAD_DIM // 8) * Q_SCALE)
    ka = _rope(head_norm(seg("ka"), gk_ref[...]), tab(0), tab(1), tab(2), HEAD_DIM // 8)
    ka_ref[...] = ka
    kab_ref[...] = ka.astype(BF16)
    va = seg("va")
    va_ref[...] = va
    put(vab_ref, va)
    put(qi_ref, _rope(seg("qi"), tab(3), tab(4), tab(5), IDX_DIM // 8))
    kiwi = _rope(seg("kiwi"), tab(6), tab(7), tab(8), IDX_DIM // 8)
    kiwi_ref[...] = kiwi
    kiwib_ref[...] = kiwi.astype(BF16)
    wi = kiwi * IDX_WEIGHT_SCALE
    if feature_major:
        w_out_ref[...] = wi.T[IDX_DIM:IDX_DIM + N_IDX_HEADS, :]
    else:
        w_out_ref[...] = wi
    put(qb_ref, seg("qb") * Q_SCALE)
    kb = seg("kb")
    kb_ref[...] = kb
    kbb_ref[...] = kb.astype(BF16)
    vb = seg("vb")
    vb_ref[...] = vb
    put(vbb_ref, vb)
    g_off = _SEG["gates"][0]
    ga_ref[...] = _dot(h, w_ref[:, g_off:g_off + d_model])
    gb_ref[...] = _dot(h, w_ref[:, g_off + d_model:g_off + 2 * d_model])


def _rope_tables(pos):
    def one(d):
        r = d // 4
        hr = r // 2
        inv = jnp.power(jnp.float32(ROPE_THETA), -jnp.arange(hr, dtype=F32) * (2.0 / r))
        ang = pos.astype(F32)[:, None] * inv[None, :]
        cos, sin = jnp.cos(ang), jnp.sin(ang)
        t = pos.shape[0]
        one_ = jnp.ones((t, d - r), F32)
        zero = jnp.zeros((t, d - r), F32)
        zh = jnp.zeros((t, hr), F32)
        c = jnp.concatenate([cos, cos, one_], axis=1)
        s1 = jnp.concatenate([-sin, zh, zero], axis=1)
        s2 = jnp.concatenate([zh, sin, zero], axis=1)
        return c, s1, s2

    c64, s164, s264 = one(HEAD_DIM)
    c32, s132, s232 = one(IDX_DIM)
    t = pos.shape[0]
    rep = lambda a, d: jnp.tile(a, (1, LANES // d))
    padc = jnp.ones((t, LANES - IDX_DIM), F32)
    padz = jnp.zeros((t, LANES - IDX_DIM), F32)
    return jnp.concatenate([
        rep(c64, HEAD_DIM), rep(s164, HEAD_DIM), rep(s264, HEAD_DIM),
        rep(c32, IDX_DIM), rep(s132, IDX_DIM), rep(s232, IDX_DIM),
        jnp.concatenate([c32, padc], axis=1), jnp.concatenate([s132, padz], axis=1),
        jnp.concatenate([s232, padz], axis=1)], axis=1)


def _project(x, pos, norm_mix, w_in_p, g_qa, g_ka):
    b, t, d_model = x.shape
    n = b * t
    tm = min(TOKEN_TILE, n)
    assert n % tm == 0 and (t % tm == 0 or tm % t == 0)
    feature_major = t % tm == 0
    tables = _rope_tables(pos)
    if t < tm:
        tables = jnp.tile(tables, (tm // t, 1))
    tiles_per_seq = max(t // tm, 1)
    gm = _block_ones(N_HEADS_A, 1.0 / HEAD_DIM)
    gq = jnp.tile(g_qa.reshape(1, HEAD_DIM), (1, N_HEADS_A))
    gk = jnp.tile(g_ka.reshape(1, HEAD_DIM), (1, N_KV_A))

    row = lambda w: pl.BlockSpec((tm, w), lambda i: (i, 0))
    rows = lambda w, dt: (jax.ShapeDtypeStruct((n, w), dt), row(w))
    if feature_major:
        cols = lambda w, dt: (jax.ShapeDtypeStruct((b, w, t), dt),
                              pl.BlockSpec((None, w, tm), lambda i: (i // tiles_per_seq, 0, i % tiles_per_seq)))
        w_out = cols(N_IDX_HEADS, F32)
    else:
        cols = rows
        w_out = rows(LANES, F32)
    names = ["ka", "va", "kiwi", "kb", "vb", "ga", "gb", "ka_bf", "kiwi_bf", "kb_bf",
             "qa", "qi", "qb", "w", "va_bf", "vb_bf"]
    outs = [rows(KV_A, F32), rows(KV_A, F32), rows(LANES, F32), rows(WIDTH_B, F32), rows(WIDTH_B, F32),
            rows(d_model, F32), rows(d_model, F32), rows(KV_A, BF16), rows(LANES, BF16), rows(WIDTH_B, BF16),
            cols(WIDTH_A, BF16), cols(WIDTH_I, BF16), cols(WIDTH_B, BF16), w_out,
            cols(KV_A, BF16), cols(WIDTH_B, BF16)]
    res = pl.pallas_call(
        functools.partial(_proj_kernel, d_model=d_model, feature_major=feature_major),
        grid=(n // tm,),
        in_specs=[row(d_model), _const_spec((1, d_model)), _const_spec(w_in_p.shape),
                  _const_spec((1, WIDTH_A)), _const_spec((1, KV_A)), _const_spec((WIDTH_A, WIDTH_A)),
                  pl.BlockSpec((tm, 9 * LANES), lambda i: (i % tiles_per_seq, 0))],
        out_specs=[o[1] for o in outs],
        out_shape=[o[0] for o in outs],
        compiler_params=pltpu.CompilerParams(dimension_semantics=("arbitrary",),
                                             vmem_limit_bytes=VMEM_LIMIT_BYTES),
        name="proj",
    )(x.reshape(n, d_model), norm_mix.reshape(1, d_model), w_in_p, gq, gk, gm, tables)
    p = dict(zip(names, res))
    if not feature_major:
        to_cols = lambda a: a.reshape(b, t, a.shape[-1]).transpose(0, 2, 1)
        for k in ("qa", "qi", "qb", "va_bf", "vb_bf"):
            p[k] = to_cols(p[k])
        p["w"] = to_cols(p["w"][:, IDX_DIM:IDX_DIM + N_IDX_HEADS])
    return p


def _dsa_kernel(qi_ref, w_ref, qa_ref, ki_ref, ka_ref, va_ref, ones_ref, o_ref,
                sc_ref, acc_ref, kmax_ref, *, pos0, n_keys, n_sel, tq, tk, idx_bits):
    i = pl.program_id(1)
    pos_first = pos0 + i * tq
    lim_max = jnp.minimum(((pos_first + tq - 1) // CHUNK + 1) * CHUNK, n_keys)
    n_tiles = (lim_max + tk - 1) // tk
    lane_pos = pos_first + lax.broadcasted_iota(I32, (1, tq), 1)
    lim_row = jnp.minimum((lax.shift_right_arithmetic(lane_pos, int(np.log2(CHUNK))) + 1) * CHUNK, n_keys)
    neg_inf = jnp.float32(-jnp.inf)

    n_super = (n_tiles + SUPER - 1) // SUPER
    row_iota = lax.broadcasted_iota(I32, (tk, tq), 0)

    def lanes_by_head(ref, first, n_heads, width):
        return jnp.concatenate([ref[(first + h) * width:(first + h + 1) * width, :] for h in range(n_heads)],
                               axis=1)

    qi = lanes_by_head(qi_ref, 0, N_IDX_HEADS, IDX_DIM)
    qi = jnp.concatenate([qi, jnp.zeros((LANES - IDX_DIM, N_IDX_HEADS * tq), BF16)], axis=0)
    blank = jnp.zeros((HEAD_DIM, GROUP_A * tq), BF16)
    qa = []
    for g in range(N_KV_A):
        parts = [blank] * N_KV_A
        parts[g] = lanes_by_head(qa_ref, g * GROUP_A, GROUP_A, HEAD_DIM)
        qa.append(jnp.concatenate(parts, axis=0))
    w = w_ref[...]

    def fold_rows(x, op):
        out = x[0:SUBLANES]
        for r in range(1, x.shape[0] // SUBLANES):
            out = op(out, x[r * SUBLANES:(r + 1) * SUBLANES])
        return out

    def score_tiles(js, carry):
        smax, smin = carry
        for c in range(SUPER):
            r0 = pl.multiple_of((js * SUPER + c) * tk, tk)
            lg = _dot(ki_ref[pl.ds(r0, tk), :], qi)
            sc = jnp.zeros((tk, tq), F32)
            for h in range(N_IDX_HEADS):
                sc = sc + jnp.maximum(lg[:, h * tq:(h + 1) * tq], 0.0) * w[h:h + 1, :]
            adm = row_iota + r0 < lim_row
            masked = jnp.where(adm, sc, neg_inf)
            sc_ref[pl.ds(r0, tk), :] = masked
            smax = jnp.maximum(smax, fold_rows(masked, jnp.maximum))
            smin = jnp.minimum(smin, fold_rows(jnp.where(adm, sc, -neg_inf), jnp.minimum))
        return smax, smin

    smax, smin = lax.fori_loop(0, n_super, score_tiles,
                               (jnp.full((SUBLANES, tq), neg_inf, F32), jnp.full((SUBLANES, tq), -neg_inf, F32)))
    smax = jnp.max(smax, axis=0, keepdims=True)
    smin = jnp.min(smin, axis=0, keepdims=True)

    n_steps = (n_tiles + 1) // 2
    tk2 = 2 * tk
    row_iota2 = lax.broadcasted_iota(I32, (tk2, tq), 0)

    def count(*preds):
        def body(j, accs):
            r0 = pl.multiple_of(j * tk2, tk2)
            s = sc_ref[pl.ds(r0, tk2), :]
            out = []
            for pred, acc in zip(preds, accs):
                hit = jnp.where(pred(s, r0), 1, 0).astype(I32)
                for r in range(tk2 // COUNT_ROWS):
                    acc = acc + hit[r * COUNT_ROWS:(r + 1) * COUNT_ROWS]
                out.append(acc)
            return tuple(out)
        accs = lax.fori_loop(0, n_steps, body, (jnp.zeros((COUNT_ROWS, tq), I32),) * len(preds))
        return [jnp.sum(a, axis=0, keepdims=True) for a in accs]

    def to_key(x):
        bits = pltpu.bitcast(x, I32)
        return bits ^ (lax.shift_right_arithmetic(bits, 31) & 0x7FFFFFFF)

    def from_key(k):
        return pltpu.bitcast(k ^ (lax.shift_right_arithmetic(k, 31) & 0x7FFFFFFF), F32)

    k_f = jnp.float32(n_sel)
    n_adm = lim_row
    zero = jnp.zeros((1, tq), F32)
    fge0, fgt0 = count(lambda s, r0: s >= zero, lambda s, r0: s > zero)
    take_all = n_adm <= n_sel
    zero_cut = fgt0 == n_sel
    zero_tie = jnp.logical_and(fgt0 < n_sel, fge0 >= n_sel)
    pos_side = fgt0 > n_sel
    neg_side = fge0 < n_sel
    as_i = lambda m: jnp.where(m, 1, 0).astype(I32)
    state = dict(
        it=jnp.int32(0),
        lo=jnp.where(pos_side, 0.0, smin), flo=jnp.where(pos_side, fge0, n_adm),
        hi=jnp.where(neg_side, 0.0, from_key(to_key(smax) + 1)), fhi=jnp.where(neg_side, fge0, 0),
        wlo=jnp.ones((1, tq), F32), whi=jnp.ones((1, tq), F32), last=jnp.zeros((1, tq), I32),
        stuck=jnp.zeros((1, tq), I32),
        done=as_i(jnp.logical_or(take_all, jnp.logical_or(zero_cut, zero_tie))),
        thr=jnp.where(take_all, neg_inf, 0.0),
        tie=as_i(jnp.logical_and(zero_tie, jnp.logical_not(take_all))),
        need=n_sel - fgt0)

    def searching(st):
        return jnp.logical_and(jnp.min(st["done"]) == 0, st["it"] < SEARCH_CAP)

    def search_step(st):
        lo, hi, flo, fhi = st["lo"], st["hi"], st["flo"], st["fhi"]
        flo_f, fhi_f = flo.astype(F32), fhi.astype(F32)
        narrow = flo - fhi <= NARROW_BRACKET
        glo = jnp.where(narrow, flo_f - k_f, jnp.log(flo_f + 0.5) - jnp.log(k_f)) * st["wlo"]
        ghi = jnp.where(narrow, k_f - fhi_f, jnp.log(k_f) - jnp.log(fhi_f + 0.5)) * st["whi"]
        frac = jnp.clip(glo / (glo + ghi), 0.005, 0.995)
        cand = lo + (hi - lo) * frac
        klo, khi, kc = to_key(lo), to_key(hi), to_key(cand)
        adjacent = klo + 1 >= khi
        mid = (lax.shift_right_arithmetic(klo, 1) + lax.shift_right_arithmetic(khi, 1) + (klo & khi & 1))
        outside = jnp.logical_or(jnp.logical_or(kc <= klo, kc >= khi), (st["it"] & FORCE_MASK) == FORCE_MASK)
        cand = jnp.where(jnp.logical_and(outside, jnp.logical_not(adjacent)), from_key(mid), cand)
        (cnt,) = count(lambda s, r0: s >= cand)
        active = st["done"] == 0
        up = jnp.logical_and(active, cnt > n_sel)
        dn = jnp.logical_and(active, cnt < n_sel)
        hit = jnp.logical_and(active, cnt == n_sel)
        stuck = jnp.where(adjacent, st["stuck"] + 1, 0)
        tied = jnp.logical_and(jnp.logical_and(active, jnp.logical_not(hit)), stuck > SUB_ULP_STEPS)
        last = st["last"]
        new_lo, new_fhi = jnp.where(up, cand, lo), jnp.where(dn, cnt, fhi)
        return dict(
            it=st["it"] + 1,
            lo=new_lo, flo=jnp.where(up, cnt, flo),
            hi=jnp.where(dn, cand, hi), fhi=new_fhi,
            wlo=jnp.where(dn, jnp.where(last == -1, st["wlo"] * 0.5, 1.0), jnp.where(up, 1.0, st["wlo"])),
            whi=jnp.where(up, jnp.where(last == 1, st["whi"] * 0.5, 1.0), jnp.where(dn, 1.0, st["whi"])),
            last=jnp.where(up, 1, jnp.where(dn, -1, last)),
            stuck=stuck,
            done=jnp.where(jnp.logical_or(hit, tied), 1, st["done"]),
            thr=jnp.where(hit, cand, jnp.where(tied, new_lo, st["thr"])),
            tie=jnp.where(tied, 1, st["tie"]),
            need=jnp.where(tied, n_sel - new_fhi, st["need"]))

    state = lax.while_loop(searching, search_step, state)
    thr, tie, need = state["thr"], state["tie"], state["need"]

    def tie_cut():
        def step(t, c0):
            cand = c0 | lax.shift_left(jnp.int32(1), idx_bits - 1 - t)
            (cnt,) = count(lambda s, r0: jnp.where(s == thr, row_iota2 + r0, cand) < cand)
            return jnp.where(cnt < need, cand, c0)
        return lax.fori_loop(0, idx_bits, step, jnp.zeros((1, tq), I32))

    any_tie = jnp.max(tie) > 0
    cut = lax.cond(any_tie, tie_cut, lambda: jnp.zeros((1, tq), I32))
    cut = jnp.where(jnp.logical_or(take_all, zero_cut), -1, jnp.where(tie > 0, cut, jnp.int32(2 ** 30)))

    acc_ref[...] = jnp.zeros(acc_ref.shape, F32)
    ones_rows = jnp.ones((BF16_ROWS, tk), BF16)

    def masked_scores(j):
        r0 = pl.multiple_of(j * tk, tk)
        sc = sc_ref[pl.ds(r0, tk), :]
        bias = jnp.where(sc > thr, 0.0,
                         jnp.where(sc == thr, jnp.where(row_iota + r0 <= cut, 0.0, MASKED), MASKED))
        bias = jnp.concatenate([bias] * GROUP_A, axis=1)
        k = ka_ref[pl.ds(r0, tk), :]
        return [_dot(k, qa[g]) + bias for g in range(N_KV_A)]

    def values(g, j):
        v = va_ref[g * HEAD_DIM:(g + 1) * HEAD_DIM, pl.ds(pl.multiple_of(j * tk, tk), tk)]
        return jnp.concatenate([v, ones_rows], axis=0)

    def attend_bounded():
        def tiles(js, _):
            js4 = [js * SUPER + c for c in range(SUPER)]
            scores = [masked_scores(j) for j in js4]
            probs = [[jnp.exp2(s).astype(BF16) for s in per_group] for per_group in scores]
            for g in range(N_KV_A):
                pv = _dot(values(g, js4[0]), probs[0][g])
                for c in range(1, SUPER):
                    pv = pv + _dot(values(g, js4[c]), probs[c][g])
                acc_ref[g] += pv
            return 0
        lax.fori_loop(0, n_super, tiles, 0)

    def attend_online():
        def tile(j, ms):
            new_ms = []
            for g, s in enumerate(masked_scores(j)):
                m_new = jnp.maximum(ms[g], jnp.max(s, axis=0, keepdims=True))
                p = jnp.exp2(s - m_new).astype(BF16)
                acc_ref[g] = acc_ref[g] * jnp.exp2(ms[g] - m_new) + _dot(values(g, j), p)
                new_ms.append(m_new)
            return tuple(new_ms)
        m0 = jnp.full((1, GROUP_A * tq), MASKED, F32)
        lax.fori_loop(0, n_tiles, tile, (m0,) * N_KV_A)

    @pl.when(i == 0)
    def _():
        kmax_ref[...] = jnp.broadcast_to(_max_head_norms(ka_ref, ones_ref, SUPER * tk), kmax_ref.shape)

    score_bound = jnp.float32(0.0)
    for g in range(N_KV_A):
        qf = qa[g].astype(F32)
        qn = jnp.sqrt(jnp.sum(qf * qf, axis=0, keepdims=True))
        score_bound = jnp.maximum(score_bound, jnp.max(qn * kmax_ref[0:1, g * HEAD_DIM:g * HEAD_DIM + 1]))
    lax.cond(score_bound <= SAFE_EXP2, attend_bounded, attend_online)
    for g in range(N_KV_A):
        out = acc_ref[g, :HEAD_DIM] / acc_ref[g, HEAD_DIM:HEAD_DIM + 1]
        for hh in range(GROUP_A):
            r = (g * GROUP_A + hh) * HEAD_DIM
            o_ref[r:r + HEAD_DIM, :] = out[:, hh * tq:(hh + 1) * tq].astype(o_ref.dtype)


def _dsa(qi_t, w_t, qa_t, kiwi, ka, va_t, *, pos0, n_keys, n_sel, tq, tk):
    b, _, t = qa_t.shape
    lp = ka.shape[1]
    kern = functools.partial(_dsa_kernel, pos0=pos0, n_keys=n_keys, n_sel=n_sel, tq=tq, tk=tk,
                             idx_bits=int(np.ceil(np.log2(lp))))
    once = pl.Buffered(1)
    per_q = lambda rows: pl.BlockSpec((None, rows, tq), lambda bi, i: (bi, 0, i))
    per_b = lambda s: pl.BlockSpec((None,) + s, lambda bi, i: (bi, 0, 0), pipeline_mode=once)
    return pl.pallas_call(
        kern,
        grid=(b, t // tq),
        in_specs=[per_q(WIDTH_I), per_q(N_IDX_HEADS), per_q(WIDTH_A),
                  per_b((lp, LANES)), per_b((lp, KV_A)), per_b((KV_A, lp)), _const_spec((KV_A, KV_A))],
        out_specs=per_q(WIDTH_A),
        out_shape=jax.ShapeDtypeStruct((b, WIDTH_A, t), BF16),
        scratch_shapes=[pltpu.VMEM((lp, tq), F32),
                        pltpu.VMEM((N_KV_A, HEAD_DIM + BF16_ROWS, GROUP_A * tq), F32),
                        pltpu.VMEM((SUBLANES, KV_A), F32)],
        compiler_params=pltpu.CompilerParams(dimension_semantics=("arbitrary", "arbitrary"),
                                             vmem_limit_bytes=VMEM_LIMIT_BYTES),
        name="dsa",
    )(qi_t, w_t, qa_t, kiwi, ka, va_t, _block_ones(N_KV_A))


def _stick_kernel(q_ref, k_ref, v_ref, tri_ref, ones_ref, o_ref, acc_ref, kmax_ref,
                  *, pos0, tq, tk, heads):
    i = pl.program_id(2)

    @pl.when(i == 0)
    def _():
        kmax_ref[...] = jnp.broadcast_to(_max_head_norms(k_ref, ones_ref, SUPER * tk), kmax_ref.shape)

    blank = jnp.zeros((HEAD_DIM, tq), BF16)
    q, bounds = [], []
    for h in range(heads):
        qh = q_ref[h * HEAD_DIM:(h + 1) * HEAD_DIM, :]
        parts = [blank] * heads
        parts[h] = qh
        q.append(jnp.concatenate(parts, axis=0))
        qf = qh.astype(F32)
        qn = jnp.sqrt(jnp.sum(qf * qf, axis=0, keepdims=True))
        bounds.append(qn * kmax_ref[0:1, h * HEAD_DIM:h * HEAD_DIM + 1] * STICK_BOUND_SLACK)
    pos_first = pos0 + i * tq
    pos_row = pos_first + lax.broadcasted_iota(I32, (1, tq), 1)
    row_iota = lax.broadcasted_iota(I32, (tk, tq), 0)
    j_first = jnp.maximum(pos_first + tq - 2, 0) // tk
    tri = tri_ref[...]
    acc_ref[...] = jnp.zeros(acc_ref.shape, F32)

    def cond(state):
        j, live, _ = state
        return jnp.logical_and(j >= 0, live > 0)

    def body(state):
        j, _, carries = state
        r0 = pl.multiple_of(j * tk, tk)
        vis = row_iota + r0 < pos_row
        hs = range(heads)
        k = k_ref[pl.ds(r0, tk), :]
        z = [_dot(k, q[h]) for h in hs]
        softplus = [jnp.maximum(z[h], 0.0) + jnp.log2(1.0 + jnp.exp2(-jnp.abs(z[h]))) for h in hs]
        log_keep = [jnp.where(vis, -softplus[h], 0.0) for h in hs]
        incl = [_dot_hilo_left(tri, log_keep[h]) for h in hs]
        a = [jnp.where(vis, jnp.exp2(z[h] + incl[h] + carries[h]), 0.0).astype(BF16) for h in hs]
        pv = [_dot(v_ref[h * HEAD_DIM:(h + 1) * HEAD_DIM, pl.ds(r0, tk)], a[h]) for h in hs]
        for h in hs:
            acc_ref[h] += pv[h]
        new_carries = [carries[h] + incl[h][0:1, :] for h in hs]
        worst = bounds[0] + new_carries[0]
        for h in range(1, heads):
            worst = jnp.maximum(worst, bounds[h] + new_carries[h])
        live = (jnp.max(worst) >= STICK_CUTOFF).astype(I32)
        return j - 1, live, tuple(new_carries)

    init = (j_first, jnp.int32(1), (jnp.zeros((1, tq), F32),) * heads)
    lax.while_loop(cond, body, init)
    for h in range(heads):
        o_ref[h * HEAD_DIM:(h + 1) * HEAD_DIM, :] = acc_ref[h].astype(o_ref.dtype)


def _stick(q_t, k, v_t, *, pos0, tq, tk):
    b, width, t = q_t.shape
    lp = k.shape[1]
    hb = STICK_HEADS
    wb = hb * HEAD_DIM
    assert width % wb == 0
    tri = jnp.asarray(np.triu(np.ones((tk, tk), np.float32)), BF16)
    kern = functools.partial(_stick_kernel, pos0=pos0, tq=tq, tk=tk, heads=hb)
    once = pl.Buffered(1)
    return pl.pallas_call(
        kern,
        grid=(b, width // wb, t // tq),
        in_specs=[pl.BlockSpec((None, wb, tq), lambda bi, h, i: (bi, h, i)),
                  pl.BlockSpec((None, lp, wb), lambda bi, h, i: (bi, 0, h), pipeline_mode=once),
                  pl.BlockSpec((None, wb, lp), lambda bi, h, i: (bi, h, 0), pipeline_mode=once),
                  _const_spec((tk, tk)), _const_spec((wb, wb))],
        out_specs=pl.BlockSpec((None, wb, tq), lambda bi, h, i: (bi, h, i)),
        out_shape=jax.ShapeDtypeStruct((b, width, t), BF16),
        scratch_shapes=[pltpu.VMEM((hb, HEAD_DIM, tq), F32), pltpu.VMEM((SUBLANES, wb), F32)],
        compiler_params=pltpu.CompilerParams(dimension_semantics=("arbitrary",) * 3,
                                             vmem_limit_bytes=VMEM_LIMIT_BYTES),
        name="stick",
    )(q_t, k, v_t, tri, _block_ones(hb))


def _rms(x, g):
    ms = jnp.mean(x * x, axis=-1, keepdims=True)
    return (x * lax.rsqrt(ms + EPS)) * g


def _tail_kernel(x_ref, oa_ref, ob_ref, ga_ref, gb_ref, p_ref, pa_ref, pb_ref, wo_ref,
                 nf_ref, wu_ref, wd_ref, np_ref, wp_ref, wg_ref, y_ref, *, ff_chunk, feature_major):
    def rows(ref):
        return ref[...].astype(F32).T.astype(BF16) if feature_major else ref[...]

    mix = (jax.nn.sigmoid(ga_ref[...]) * _dot(rows(oa_ref), pa_ref[...])
           + jax.nn.sigmoid(gb_ref[...]) * _dot(rows(ob_ref), pb_ref[...]))
    x = x_ref[...] + _dot(mix.astype(BF16), wo_ref[...])
    h = _rms(x, nf_ref[...]).astype(BF16)
    d_ff = wu_ref.shape[1]
    ffn = jnp.zeros_like(x)
    for c in range(d_ff // ff_chunk):
        u = jnp.maximum(_dot(h, wu_ref[:, c * ff_chunk:(c + 1) * ff_chunk]), 0.0)
        ffn = ffn + _dot((u * u).astype(BF16), wd_ref[c * ff_chunk:(c + 1) * ff_chunk, :])
    x = x + ffn
    h = _rms(x, np_ref[...]).astype(BF16)
    y_ref[...] = x + _dot(p_ref[...].astype(BF16), wp_ref[...]) * jax.nn.sigmoid(_dot(h, wg_ref[...]))


def _tail(x, o_a, o_b, ga, gb, p, proj_a, proj_b, w_out, norm_ffn, w_up, w_down,
          norm_ple, w_ple, w_gate):
    b, t, d_model = x.shape
    n = b * t
    tm = min(TOKEN_TILE, n)
    assert n % tm == 0
    feature_major = t % tm == 0
    d_ff = w_up.shape[1]
    row = lambda w: pl.BlockSpec((tm, w), lambda i: (i, 0))
    if feature_major:
        tiles_per_seq = t // tm
        branch = lambda w: pl.BlockSpec((None, w, tm), lambda i: (i // tiles_per_seq, 0, i % tiles_per_seq))
    else:
        o_a, o_b = (o.transpose(0, 2, 1).reshape(n, o.shape[1]) for o in (o_a, o_b))
        branch = row
    weights = (proj_a, proj_b, w_out, norm_ffn.reshape(1, d_model), w_up, w_down,
               norm_ple.reshape(1, d_model), w_ple, w_gate)
    return pl.pallas_call(
        functools.partial(_tail_kernel, ff_chunk=min(1024, d_ff), feature_major=feature_major),
        grid=(n // tm,),
        in_specs=[row(d_model), branch(WIDTH_A), branch(WIDTH_B), row(d_model), row(d_model),
                  row(p.shape[-1])] + [_const_spec(w.shape) for w in weights],
        out_specs=row(d_model),
        out_shape=jax.ShapeDtypeStruct((n, d_model), F32),
        compiler_params=pltpu.CompilerParams(dimension_semantics=("arbitrary",),
                                             vmem_limit_bytes=VMEM_LIMIT_BYTES),
        name="tail",
    )(x.reshape(n, d_model), o_a, o_b, ga, gb, p.reshape(n, p.shape[-1]), *weights).reshape(b, t, d_model)


def _layer(x, p, pos0, caches, wts):
    (norm_mix, w_in_p, g_qa, g_ka, proj_a, proj_b, w_out, norm_ffn, w_up, w_down,
     norm_ple, w_ple, w_gate) = wts
    b, t, d_model = x.shape
    pos = pos0 + jnp.arange(t, dtype=I32)
    pr = _project(x, pos, norm_mix, w_in_p, g_qa, g_ka)

    per_seq = lambda a: a.reshape(b, t, a.shape[-1])
    new_state = (per_seq(pr["ka"]).reshape(b, t, N_KV_A, HEAD_DIM),
                 per_seq(pr["va"]).reshape(b, t, N_KV_A, HEAD_DIM),
                 per_seq(pr["kiwi"])[:, :, :IDX_DIM],
                 per_seq(pr["kb"]).reshape(b, t, N_HEADS_B, HEAD_DIM),
                 per_seq(pr["vb"]).reshape(b, t, N_HEADS_B, HEAD_DIM))
    kiwi, ka, kb = per_seq(pr["kiwi_bf"]), per_seq(pr["ka_bf"]), per_seq(pr["kb_bf"])
    va_t, vb_t = pr["va_bf"], pr["vb_bf"]
    tk = KEY_TILE
    if caches is not None:
        c_ka, c_va, c_ki, c_kb, c_vb = caches
        past = c_ka.shape[1]
        flat = lambda c: c.reshape(b, past, -1).astype(BF16)
        ki_lanes = jnp.pad(flat(c_ki), ((0, 0), (0, 0), (0, LANES - IDX_DIM)))
        kiwi = jnp.concatenate([ki_lanes, kiwi], axis=1)
        ka = jnp.concatenate([flat(c_ka), ka], axis=1)
        kb = jnp.concatenate([flat(c_kb), kb], axis=1)
        va_t = jnp.concatenate([flat(c_va).transpose(0, 2, 1), va_t], axis=2)
        vb_t = jnp.concatenate([flat(c_vb).transpose(0, 2, 1), vb_t], axis=2)
    n_keys = ka.shape[1]
    lp = -(-n_keys // (SUPER * tk)) * (SUPER * tk)
    if lp != n_keys:
        pad_rows = lambda a: jnp.pad(a, ((0, 0), (0, lp - n_keys), (0, 0)))
        pad_cols = lambda a: jnp.pad(a, ((0, 0), (0, 0), (0, lp - n_keys)))
        kiwi, ka, kb, va_t, vb_t = pad_rows(kiwi), pad_rows(ka), pad_rows(kb), pad_cols(va_t), pad_cols(vb_t)
    n_sel = min(TOPK_MAX, n_keys // 4)

    o_a = _dsa(pr["qi"], pr["w"], pr["qa"], kiwi, ka, va_t, pos0=pos0, n_keys=n_keys, n_sel=n_sel,
               tq=min(Q_TILE_A, t), tk=tk)
    o_b = _stick(pr["qb"], kb, vb_t, pos0=pos0, tq=min(Q_TILE_B, t), tk=tk)
    y = _tail(x, o_a, o_b, pr["ga"], pr["gb"], p, proj_a, proj_b, w_out, norm_ffn,
              w_up, w_down, norm_ple, w_ple, w_gate)
    return y, new_state


def kernel(x_prompt, x_sample, cache_a_k, cache_a_v, cache_a_kidx, cache_b_k, cache_b_v, p_prompt, p_sample, norm_mix, w_in, g_qa, g_ka, proj_a, proj_b, w_out, norm_ffn, w_up, w_down, norm_ple, w_ple, w_ple_gate):
    depth = w_in.shape[0]
    past = cache_a_k.shape[2]
    d_model = x_prompt.shape[-1]
    split = _SEG["kiwi"][0] + IDX_DIM + N_IDX_HEADS
    assert w_in.shape[2] == split + 3 * WIDTH_B + 2 * d_model
    xp, xs = x_prompt, x_sample
    outs_p, outs_s = [], []
    for i in range(depth):
        w_in_p = jnp.concatenate(
            [w_in[i, :, :split], jnp.zeros((d_model, LANES - IDX_DIM - N_IDX_HEADS), w_in.dtype),
             w_in[i, :, split:]], axis=1).astype(BF16)
        wts = (norm_mix[i], w_in_p, g_qa[i], g_ka[i], proj_a[i].astype(BF16), proj_b[i].astype(BF16),
               w_out[i].astype(BF16), norm_ffn[i], w_up[i].astype(BF16), w_down[i].astype(BF16),
               norm_ple[i], w_ple[i].astype(BF16), w_ple_gate[i].astype(BF16))
        xp, st = _layer(xp, p_prompt[i], 0, None, wts)
        outs_p.append(st)
        caches = (cache_a_k[i], cache_a_v[i], cache_a_kidx[i], cache_b_k[i], cache_b_v[i])
        xs, st = _layer(xs, p_sample[i], past, caches, wts)
        outs_s.append(st)
    stack = lambda outs, k: jnp.stack([o[k] for o in outs], 0)
    return ((xp, xs) + tuple(stack(outs_p, k) for k in range(5))
            + tuple(stack(outs_s, k) for k in range(5)))
```
